```python
import jax
import jax.numpy as jnp
from jax import lax
import numpy as np

D_MODEL = 4096
BATCH = 4
SEQ = 2048
DEPTH = 2
DEC_BATCH = 8
DEC_SEQ = 8
PAST_LEN = 16384
PAGE_SIZE = 128

HEAD_DIM = 128
H_A = D_MODEL // 2 // HEAD_DIM
KV_A = 4
IDX_HEADS = 32
IDX_DIM = 64
DSA_TOPK = 256
H_B = D_MODEL // 2 // HEAD_DIM
KV_B = 4
MOBA_BLOCK = 256
MOBA_TOPK = 3
N_GROUPS = 4
EXPERTS_PER_GROUP = 8
N_EXPERTS = N_GROUPS * EXPERTS_PER_GROUP
EXPERT_TOPK = 2
D_EXPERT = 1024
PLE_DIM = 256
ROPE_THETA = 10000.0
LN_EPS = 1e-5
Q_BLOCK = 32
MOE_ROW_BLOCK = 128
IN_WIDTHS = (H_A * HEAD_DIM, KV_A * HEAD_DIM, KV_A * HEAD_DIM, IDX_HEADS * IDX_DIM, IDX_DIM, IDX_HEADS,
             H_B * HEAD_DIM, KV_B * HEAD_DIM, KV_B * HEAD_DIM)
IN_WIDTH = sum(IN_WIDTHS)

kernel_name = "hybrid_dsa_moba_hmoe_decoder_step"


def layer_norm(x, g, b):
    xf = x.astype(jnp.float32)
    mu = jnp.mean(xf, axis=-1, keepdims=True)
    var = jnp.mean(jnp.square(xf - mu), axis=-1, keepdims=True)
    y = (xf - mu) * lax.rsqrt(var + LN_EPS)
    return (y * g.astype(jnp.float32) + b.astype(jnp.float32)).astype(x.dtype)


def rope(x, pos):
    d = x.shape[-1]
    half = d // 2
    inv = ROPE_THETA ** (-jnp.arange(half, dtype=jnp.float32) * 2.0 / d)
    ang = pos.astype(jnp.float32)[:, None] * inv[None, :]
    cos = jnp.cos(ang)[None, :, None, :].astype(x.dtype)
    sin = jnp.sin(ang)[None, :, None, :].astype(x.dtype)
    x1, x2 = x[..., :half], x[..., half:]
    return jnp.concatenate([x1 * cos - x2 * sin, x1 * sin + x2 * cos], axis=-1)


def dense_rows(arr):
    def fetch(pos, grp):
        nd = max(pos.ndim, grp.ndim)
        b = jnp.arange(arr.shape[0]).reshape((-1,) + (1,) * (nd - 1))
        return arr[b, pos, grp]
    return fetch


def paged_rows(pool, page_table, new):
    page = pool.shape[1]
    past = page_table.shape[1] * page
    def fetch(pos, grp):
        nd = max(pos.ndim, grp.ndim)
        b = jnp.arange(new.shape[0]).reshape((-1,) + (1,) * (nd - 1))
        pc = jnp.minimum(pos, past - 1)
        old = pool[page_table[b, pc // page], pc % page, grp]
        fresh = new[b, jnp.clip(pos - past, 0, new.shape[1] - 1), grp]
        return jnp.where((pos >= past)[..., None], fresh, old)
    return fetch


def gather_pages(pool, page_table):
    g = pool[page_table]
    return g.reshape((g.shape[0], g.shape[1] * g.shape[2]) + g.shape[3:])


def over_query_blocks(fn, q_arrays, pos):
    T = pos.shape[0]
    qb = Q_BLOCK if T % Q_BLOCK == 0 else T
    nb = T // qb
    if nb == 1:
        return fn(q_arrays, pos)
    split = tuple(jnp.swapaxes(a.reshape((a.shape[0], nb, qb) + a.shape[2:]), 0, 1) for a in q_arrays)
    out = lax.map(lambda blk: fn(blk[0], blk[1]), (split, pos.reshape(nb, qb)))
    out = jnp.swapaxes(out, 0, 1)
    return out.reshape((out.shape[0], T) + out.shape[3:])


def project_heads(h, w_in_l, pos):
    B, T, _ = h.shape
    cuts = np.cumsum(IN_WIDTHS)[:-1].tolist()
    qa, ka, va, qi, ki, wi, qb, kb, vb = jnp.split(h @ w_in_l, cuts, axis=-1)
    qa = rope(qa.reshape(B, T, H_A, HEAD_DIM), pos)
    ka = rope(ka.reshape(B, T, KV_A, HEAD_DIM), pos)
    va = va.reshape(B, T, KV_A, HEAD_DIM)
    qi = rope(qi.reshape(B, T, IDX_HEADS, IDX_DIM), pos)
    ki = rope(ki.reshape(B, T, 1, IDX_DIM), pos)[:, :, 0]
    wi = wi * (IDX_HEADS ** -0.5 * IDX_DIM ** -0.5)
    qb = rope(qb.reshape(B, T, H_B, HEAD_DIM), pos)
    kb = rope(kb.reshape(B, T, KV_B, HEAD_DIM), pos)
    vb = vb.reshape(B, T, KV_B, HEAD_DIM)
    return qa, ka, va, qi, ki, wi, qb, kb, vb


def dsa_attention(q, qi, wi, pos, ki_all, k_rows, v_rows):
    L = ki_all.shape[1]
    n_keep = min(DSA_TOPK, L // 4)
    key_pos = jnp.arange(L)
    grp = jnp.arange(KV_A)
    scale = HEAD_DIM ** -0.5

    def block(args, pos_blk):
        qb_, qib, wib = args
        B, Tq = qb_.shape[0], qb_.shape[1]
        dots = jnp.einsum('bthd,bsd->bths', qib, ki_all)
        score = jnp.einsum('bths,bth->bts', jax.nn.relu(dots).astype(jnp.float32), wib.astype(jnp.float32))
        score = jnp.where(key_pos[None, None, :] <= pos_blk[None, :, None], score, -jnp.inf)
        _, idx = lax.top_k(score, n_keep)
        valid = idx <= pos_blk[None, :, None]
        k = k_rows(idx[..., None], grp)
        v = v_rows(idx[..., None], grp)
        qg = qb_.reshape(B, Tq, KV_A, H_A // KV_A, HEAD_DIM)
        s = jnp.einsum('btgrd,btsgd->btgrs', qg, k).astype(jnp.float32) * scale
        s = jnp.where(valid[:, :, None, None, :], s, -jnp.inf)
        p = jax.nn.softmax(s, axis=-1).astype(v.dtype)
        o = jnp.einsum('btgrs,btsgd->btgrd', p, v)
        return o.reshape(B, Tq, H_A * HEAD_DIM)

    return over_query_blocks(block, (q, qi, wi), pos)


def moba_attention(q, pos, k_all, k_rows, v_rows):
    B, L = k_all.shape[0], k_all.shape[1]
    n_blocks = L // MOBA_BLOCK
    n_pick = min(MOBA_TOPK, n_blocks)
    grp = jnp.arange(KV_B)
    offs = jnp.arange(MOBA_BLOCK)
    rep = H_B // KV_B
    scale = HEAD_DIM ** -0.5
    means = None
    if n_pick > 0:
        means = k_all[:, :n_blocks * MOBA_BLOCK].astype(jnp.float32).reshape(
            B, n_blocks, MOBA_BLOCK, KV_B, HEAD_DIM).mean(axis=2)

    def block(args, pos_blk):
        (qb_,) = args
        Tq = qb_.shape[1]
        qg = qb_.reshape(B, Tq, KV_B, rep, HEAD_DIM)
        own = pos_blk // MOBA_BLOCK
        own_pos = own[:, None] * MOBA_BLOCK + offs[None, :]
        own_valid = own_pos <= pos_blk[:, None]
        own_idx = jnp.broadcast_to(jnp.minimum(own_pos, pos_blk[:, None]), (B, Tq, MOBA_BLOCK))[..., None]
        k_own = k_rows(own_idx, grp)
        v_own = v_rows(own_idx, grp)
        s_own = jnp.einsum('btgrd,btsgd->btgrs', qg, k_own).astype(jnp.float32) * scale
        s_own = jnp.where(own_valid[None, :, None, None, :], s_own, -jnp.inf)
        if n_pick == 0:
            p = jax.nn.softmax(s_own, axis=-1).astype(v_own.dtype)
            o = jnp.einsum('btgrs,btsgd->btgrd', p, v_own)
        else:
            gate = jnp.einsum('btgrd,bngd->btgn', qg.astype(jnp.float32), means)
            past_ok = jnp.arange(n_blocks)[None, :] < own[:, None]
            gate = jnp.where(past_ok[None, :, None, :], gate, -jnp.inf)
            _, pick = lax.top_k(gate, n_pick)
            pick_valid = pick < own[None, :, None, None]
            sel_pos = (pick[..., None] * MOBA_BLOCK + offs).reshape(B, Tq, KV_B, n_pick * MOBA_BLOCK)
            sel_valid = jnp.repeat(pick_valid, MOBA_BLOCK, axis=-1)
            k_sel = k_rows(sel_pos, grp[:, None])
            v_sel = v_rows(sel_pos, grp[:, None])
            s_sel = jnp.einsum('btgrd,btgsd->btgrs', qg, k_sel).astype(jnp.float32) * scale
            s_sel = jnp.where(sel_valid[:, :, :, None, :], s_sel, -jnp.inf)
            p = jax.nn.softmax(jnp.concatenate([s_sel, s_own], axis=-1), axis=-1).astype(v_sel.dtype)
            n_sel = n_pick * MOBA_BLOCK
            o = (jnp.einsum('btgrs,btgsd->btgrd', p[..., :n_sel], v_sel)
                 + jnp.einsum('btgrs,btsgd->btgrd', p[..., n_sel:], v_own))
        return o.reshape(B, Tq, H_B * HEAD_DIM)

    return over_query_blocks(block, (q,), pos)


def hier_moe(x2, w_rg, w_re, w_gate, w_up, w_down):
    N, D = x2.shape
    rows = jnp.arange(N)
    g_logits = (x2 @ w_rg).astype(jnp.float32)
    g_prob = jax.nn.softmax(g_logits, axis=-1)
    g_sel = jnp.argmax(g_logits, axis=-1).astype(jnp.int32)
    g_w = g_prob[rows, g_sel]
    e_logits = (x2 @ w_re).astype(jnp.float32).reshape(N, N_GROUPS, EXPERTS_PER_GROUP)[rows, g_sel]
    top_v, top_i = lax.top_k(e_logits, EXPERT_TOPK)
    gate = g_w[:, None] * jax.nn.softmax(top_v, axis=-1)
    eid = (g_sel[:, None] * EXPERTS_PER_GROUP + top_i).reshape(-1).astype(jnp.int32)
    tok = jnp.repeat(jnp.arange(N, dtype=jnp.int32), EXPERT_TOPK)
    M = N * EXPERT_TOPK
    blk = min(MOE_ROW_BLOCK, max(8, M // N_EXPERTS))
    order = jnp.argsort(eid)
    eid_s = eid[order]
    tok_s = tok[order]
    counts = jnp.bincount(eid, length=N_EXPERTS)
    start = jnp.cumsum(counts) - counts
    pcounts = (counts + blk - 1) // blk * blk
    pend = jnp.cumsum(pcounts)
    pstart = pend - pcounts
    dest_s = (pstart[eid_s] + jnp.arange(M) - start[eid_s]).astype(jnp.int32)
    n_blk = (M + blk - 1) // blk + N_EXPERTS
    R = n_blk * blk
    buf_tok = jnp.full((R,), N, jnp.int32).at[dest_s].set(tok_s)
    blk_e = jnp.minimum(jnp.searchsorted(pend, jnp.arange(n_blk) * blk, side='right'), N_EXPERTS - 1)
    xpad = jnp.concatenate([x2, jnp.zeros((1, D), x2.dtype)], axis=0)
    xb = xpad[buf_tok].reshape(n_blk, blk, D)

    def expert_rows(args):
        xr, e = args
        return (jax.nn.silu(xr @ w_gate[e]) * (xr @ w_up[e])) @ w_down[e]

    yb = lax.map(expert_rows, (xb, blk_e)).reshape(R, D)
    dest = jnp.zeros((M,), jnp.int32).at[order].set(dest_s)
    y_as = yb[dest].reshape(N, EXPERT_TOPK, D)
    return jnp.einsum('nk,nkd->nd', gate.astype(x2.dtype), y_as)


def decoder_layer(x, p_l, pos, lw, alpha, past):
    (w_in, w_out, ln1_g, ln1_b, w_rg, w_re, w_eg, w_eu, w_ed,
     ln2_g, ln2_b, w_ple, w_ple_gate, ln3_g, ln3_b) = lw
    qa, ka, va, qi, ki, wi, qb, kb, vb = project_heads(x, w_in, pos)
    if past is None:
        ki_all, kb_all = ki, kb
        ak_rows, av_rows = dense_rows(ka), dense_rows(va)
        bk_rows, bv_rows = dense_rows(kb), dense_rows(vb)
    else:
        pool_ak, pool_av, pool_ki, pool_bk, pool_bv, page_table = past
        ki_all = jnp.concatenate([gather_pages(pool_ki, page_table), ki], axis=1)
        kb_all = jnp.concatenate([gather_pages(pool_bk, page_table), kb], axis=1)
        ak_rows = paged_rows(pool_ak, page_table, ka)
        av_rows = paged_rows(pool_av, page_table, va)
        bk_rows = dense_rows(kb_all)
        bv_rows = paged_rows(pool_bv, page_table, vb)
    out_a = dsa_attention(qa, qi, wi, pos, ki_all, ak_rows, av_rows)
    out_b = moba_attention(qb, pos, kb_all, bk_rows, bv_rows)
    mix = jnp.concatenate([out_a, out_b], axis=-1) @ w_out
    x = layer_norm(alpha * x + mix, ln1_g, ln1_b)
    B, T, D = x.shape
    ffn = hier_moe(x.reshape(B * T, D), w_rg, w_re, w_eg, w_eu, w_ed).reshape(B, T, D)
    x = layer_norm(alpha * x + ffn, ln2_g, ln2_b)
    ple = jax.nn.sigmoid(x @ w_ple_gate) * (p_l.astype(x.dtype) @ w_ple)
    x = layer_norm(alpha * x + ple, ln3_g, ln3_b)
    return x, (ka, va, ki, kb, vb)


def setup_inputs(seed: int = 0) -> dict:
    key = jax.random.key(seed)
    k = jax.random.split(key, 32)
    f32 = jnp.float32
    n_pages = PAST_LEN // PAGE_SIZE
    n_used = DEC_BATCH * n_pages
    n_phys = n_used + max(1, n_used // 4)
    beta = (8.0 * DEPTH) ** -0.25

    def normal(kk, shape, scale):
        return jax.random.normal(kk, shape, f32) * scale

    v_cols = (2, 8)
    col_scale = jnp.concatenate([jnp.full((w,), beta if i in v_cols else 1.0, f32)
                                 for i, w in enumerate(IN_WIDTHS)])
    page_table = jax.random.permutation(k[0], n_phys)[:n_used].reshape(DEC_BATCH, n_pages).astype(jnp.int32)
    return {
        "x_prompt": normal(k[1], (BATCH, SEQ, D_MODEL), 1.0),
        "x_sample": normal(k[2], (DEC_BATCH, DEC_SEQ, D_MODEL), 1.0),
        "cache_a_k": normal(k[3], (DEPTH, n_phys, PAGE_SIZE, KV_A, HEAD_DIM), 1.0),
        "cache_a_v": normal(k[4], (DEPTH, n_phys, PAGE_SIZE, KV_A, HEAD_DIM), 1.0),
        "cache_a_kidx": normal(k[5], (DEPTH, n_phys, PAGE_SIZE, IDX_DIM), 1.0),
        "cache_b_k": normal(k[6], (DEPTH, n_phys, PAGE_SIZE, KV_B, HEAD_DIM), 1.0),
        "cache_b_v": normal(k[7], (DEPTH, n_phys, PAGE_SIZE, KV_B, HEAD_DIM), 1.0),
        "page_table": page_table,
        "p_prompt": normal(k[8], (DEPTH, BATCH, SEQ, PLE_DIM), 1.0),
        "p_sample": normal(k[9], (DEPTH, DEC_BATCH, DEC_SEQ, PLE_DIM), 1.0),
        "ln_emb_g": 1.0 + normal(k[10], (D_MODEL,), 0.02),
        "ln_emb_b": normal(k[11], (D_MODEL,), 0.02),
        "w_in": normal(k[12], (DEPTH, D_MODEL, IN_WIDTH), D_MODEL ** -0.5) * col_scale,
        "w_out": normal(k[13], (DEPTH, D_MODEL, D_MODEL), D_MODEL ** -0.5 * beta),
        "ln1_g": 1.0 + normal(k[14], (DEPTH, D_MODEL), 0.02),
        "ln1_b": normal(k[15], (DEPTH, D_MODEL), 0.02),
        "w_route_group": normal(k[16], (DEPTH, D_MODEL, N_GROUPS), D_MODEL ** -0.5),
        "w_route_expert": normal(k[17], (DEPTH, D_MODEL, N_EXPERTS), D_MODEL ** -0.5),
        "w_exp_gate": normal(k[18], (DEPTH, N_EXPERTS, D_MODEL, D_EXPERT), D_MODEL ** -0.5),
        "w_exp_up": normal(k[19], (DEPTH, N_EXPERTS, D_MODEL, D_EXPERT), D_MODEL ** -0.5),
        "w_exp_down": normal(k[20], (DEPTH, N_EXPERTS, D_EXPERT, D_MODEL), D_EXPERT ** -0.5 * beta),
        "ln2_g": 1.0 + normal(k[21], (DEPTH, D_MODEL), 0.02),
        "ln2_b": normal(k[22], (DEPTH, D_MODEL), 0.02),
        "w_ple": normal(k[23], (DEPTH, PLE_DIM, D_MODEL), PLE_DIM ** -0.5 * beta),
        "w_ple_gate": normal(k[24], (DEPTH, D_MODEL, D_MODEL), D_MODEL ** -0.5),
        "ln3_g": 1.0 + normal(k[25], (DEPTH, D_MODEL), 0.02),
        "ln3_b": normal(k[26], (DEPTH, D_MODEL), 0.02),
    }


def reference(x_prompt, x_sample, cache_a_k, cache_a_v, cache_a_kidx, cache_b_k, cache_b_v, page_table,
              p_prompt, p_sample, ln_emb_g, ln_emb_b, w_in, w_out, ln1_g, ln1_b, w_route_group,
              w_route_expert, w_exp_gate, w_exp_up, w_exp_down, ln2_g, ln2_b, w_ple, w_ple_gate,
              ln3_g, ln3_b):
    alpha = (2.0 * DEPTH) ** 0.25
    past_len = page_table.shape[1] * cache_a_k.shape[2]
    pos_p = jnp.arange(x_prompt.shape[1], dtype=jnp.int32)
    pos_s = past_len + jnp.arange(x_sample.shape[1], dtype=jnp.int32)
    h_p = layer_norm(x_prompt, ln_emb_g, ln_emb_b)
    h_s = layer_norm(x_sample, ln_emb_g, ln_emb_b)
    rows_p, rows_s = [], []
    for l in range(DEPTH):
        lw = (w_in[l], w_out[l], ln1_g[l], ln1_b[l], w_route_group[l], w_route_expert[l],
              w_exp_gate[l], w_exp_up[l], w_exp_down[l], ln2_g[l], ln2_b[l], w_ple[l], w_ple_gate[l],
              ln3_g[l], ln3_b[l])
        h_p, r_p = decoder_layer(h_p, p_prompt[l], pos_p, lw, alpha, None)
        past = (cache_a_k[l], cache_a_v[l], cache_a_kidx[l], cache_b_k[l], cache_b_v[l], page_table)
        h_s, r_s = decoder_layer(h_s, p_sample[l], pos_s, lw, alpha, past)
        rows_p.append(r_p)
        rows_s.append(r_s)
    a_k_p = jnp.stack([r[0] for r in rows_p])
    a_v_p = jnp.stack([r[1] for r in rows_p])
    a_kidx_p = jnp.stack([r[2] for r in rows_p])
    b_k_p = jnp.stack([r[3] for r in rows_p])
    b_v_p = jnp.stack([r[4] for r in rows_p])
    a_k_s = jnp.stack([r[0] for r in rows_s])
    a_v_s = jnp.stack([r[1] for r in rows_s])
    a_kidx_s = jnp.stack([r[2] for r in rows_s])
    b_k_s = jnp.stack([r[3] for r in rows_s])
    b_v_s = jnp.stack([r[4] for r in rows_s])
    return (h_p, h_s, a_k_p, a_v_p, a_kidx_p, b_k_p, b_v_p, a_k_s, a_v_s, a_kidx_s, b_k_s, b_v_s)
```

```python
import functools

import jax
import jax.numpy as jnp
from jax import lax
from jax.experimental import pallas as pl
from jax.experimental.pallas import tpu as pltpu

HEAD_DIM = 128
KV_A = 4
IDX_HEADS = 32
IDX_DIM = 64
DSA_TOPK = 256
KV_B = 4
MOBA_BLOCK = 256
MOBA_TOPK = 3
N_GROUPS = 4
EXPERTS_PER_GROUP = 8
EXPERT_TOPK = 2
ROPE_THETA = 10000.0
LN_EPS = 1e-5

LANE = 128
ROW_ALIGN = 256
MXU_DTYPE = jnp.bfloat16
NEG_BIAS = -1e30
INT_MIN = -2147483648
VMEM_LIMIT = 56 * 1024 * 1024
PAGES_PER_CHUNK = 16
MOE_ROWS = 128
MOE_CHUNKS = 5
MOE_FT = 256


def _cparams(*sem):
    return pltpu.CompilerParams(dimension_semantics=sem, vmem_limit_bytes=VMEM_LIMIT)


def _round_up(n, m):
    return (n + m - 1) // m * m


def _pick_tile(n, cap, mult):
    best = None
    for t in range(mult, cap + 1, mult):
        if n % t == 0:
            best = t
    assert best is not None, (n, cap, mult)
    return best


def _dot_nt(a, b):
    return lax.dot_general(a, b, (((1,), (1,)), ((), ())), preferred_element_type=jnp.float32)


def _ln_kernel(*refs, mode, alpha, router):
    it = iter(refs)
    x_ref = next(it)
    a_ref = next(it) if mode == "add" else None
    g_ref = next(it) if mode == "ple" else None
    p_ref = next(it) if mode == "ple" else None
    gam_ref, bet_ref = next(it), next(it)
    wh_ref = next(it) if router else None
    wl_ref = next(it) if router else None
    o32_ref, o16_ref = next(it), next(it)
    lg_ref = next(it) if router else None

    v = x_ref[...]
    if mode == "add":
        v = alpha * v + a_ref[...]
    elif mode == "ple":
        v = alpha * v + jax.nn.sigmoid(g_ref[...]) * p_ref[...]
    mu = jnp.mean(v, axis=-1, keepdims=True)
    d = v - mu
    var = jnp.mean(d * d, axis=-1, keepdims=True)
    y = d * lax.rsqrt(var + LN_EPS) * gam_ref[...] + bet_ref[...]
    o32_ref[...] = y
    o16_ref[...] = y.astype(o16_ref.dtype)
    if router:
        yh = y.astype(MXU_DTYPE)
        yl = (y - yh.astype(jnp.float32)).astype(MXU_DTYPE)
        wh = wh_ref[...]
        lg = jnp.dot(yh, wh, preferred_element_type=jnp.float32)
        lg = lg + jnp.dot(yl, wh, preferred_element_type=jnp.float32)
        lg = lg + jnp.dot(yh, wl_ref[...], preferred_element_type=jnp.float32)
        lg_ref[...] = lg


def _layer_norm(xs, gam, bet, mode, alpha=1.0, router_w=None):
    Np, D = xs[0].shape
    tr = _pick_tile(Np, 192, 16)
    row = pl.BlockSpec((tr, D), lambda i: (i, 0))
    vec = pl.BlockSpec((1, D), lambda i: (0, 0))
    in_specs = [row] * len(xs) + [vec, vec]
    args = list(xs) + [gam.reshape(1, D), bet.reshape(1, D)]
    out_shape = [jax.ShapeDtypeStruct((Np, D), jnp.float32), jax.ShapeDtypeStruct((Np, D), MXU_DTYPE)]
    out_specs = [row, row]
    if router_w is not None:
        wspec = pl.BlockSpec((D, LANE), lambda i: (0, 0))
        in_specs += [wspec, wspec]
        args += list(router_w)
        out_shape.append(jax.ShapeDtypeStruct((Np, LANE), jnp.float32))
        out_specs.append(pl.BlockSpec((tr, LANE), lambda i: (i, 0)))
    return pl.pallas_call(
        functools.partial(_ln_kernel, mode=mode, alpha=alpha, router=router_w is not None),
        grid=(Np // tr,), in_specs=in_specs, out_specs=out_specs, out_shape=out_shape,
        compiler_params=_cparams("parallel"), name="ln_" + mode)(*args)


def _mm_kernel(*refs, n_x, mode, slice_major):
    x_refs = refs[:n_x]
    w_refs = refs[n_x:2 * n_x]
    rest = refs[2 * n_x:]
    if mode == "none":
        (o_ref,) = rest
    else:
        cos_ref, sin_ref, o_ref = rest
    y = jnp.dot(x_refs[0][...], w_refs[0][...], preferred_element_type=jnp.float32)
    for k in range(1, n_x):
        y = y + jnp.dot(x_refs[k][...], w_refs[k][...], preferred_element_type=jnp.float32)
    if mode == "none" and not slice_major:
        o_ref[...] = y.astype(o_ref.dtype)
        return
    tm = y.shape[0]
    if mode != "none":
        cos = cos_ref[...]
        sin = sin_ref[...]
    if mode == "rope64":
        lane = lax.broadcasted_iota(jnp.int32, (tm, LANE), 1)
        first_half = (lane % 64) < 32
    for s in range(y.shape[1] // LANE):
        yh = y[:, s * LANE:(s + 1) * LANE]
        if mode == "rope128":
            yh = yh * cos + pltpu.roll(yh, 64, 1) * sin
        elif mode == "rope64":
            partner = jnp.where(first_half, pltpu.roll(yh, 96, 1), pltpu.roll(yh, 32, 1))
            yh = yh * cos + partner * sin
        if slice_major:
            o_ref[s] = yh.astype(o_ref.dtype)
        else:
            o_ref[:, s * LANE:(s + 1) * LANE] = yh.astype(o_ref.dtype)


def _matmul(xs, ws, out_dtype, mode="none", tables=None, slice_major=False, name="mm"):
    Np = xs[0].shape[0]
    Nc = ws[0].shape[1]
    tm = _pick_tile(Np, 640, 16)
    tn = _pick_tile(Nc, 1024, LANE)
    in_specs = [pl.BlockSpec((tm, x.shape[1]), lambda j, i: (i, 0)) for x in xs]
    in_specs += [pl.BlockSpec((w.shape[0], tn), lambda j, i: (0, j)) for w in ws]
    args = list(xs) + list(ws)
    if mode != "none":
        in_specs += [pl.BlockSpec((tm, LANE), lambda j, i: (i, 0))] * 2
        args += list(tables)
    if slice_major:
        out_shape = jax.ShapeDtypeStruct((Nc // LANE, Np, LANE), out_dtype)
        out_spec = pl.BlockSpec((tn // LANE, tm, LANE), lambda j, i: (j, i, 0))
    else:
        out_shape = jax.ShapeDtypeStruct((Np, Nc), out_dtype)
        out_spec = pl.BlockSpec((tm, tn), lambda j, i: (i, j))
    return pl.pallas_call(
        functools.partial(_mm_kernel, n_x=len(xs), mode=mode, slice_major=slice_major),
        grid=(Nc // tn, Np // tm), in_specs=in_specs, out_specs=out_spec, out_shape=out_shape,
        compiler_params=_cparams("parallel", "parallel"), name=name)(*args)


def _float_keys(x):
    b = lax.bitcast_convert_type(x, jnp.int32)
    return jnp.where(b < 0, b ^ jnp.int32(0x7FFFFFFF), b)


def _kth_largest(count_ge, shape, k):
    def body(it, ans):
        cand = ans + lax.shift_left(jnp.int32(1), 31 - it)
        return jnp.where(count_ge(cand) >= k, cand, ans)
    return lax.fori_loop(0, 32, body, jnp.full(shape, INT_MIN, jnp.int32))


def _topk_lanes(gate, k):
    lane = lax.broadcasted_iota(jnp.int32, gate.shape, 1)
    sel = jnp.zeros(gate.shape, jnp.float32)
    g = gate
    for _ in range(k):
        m = jnp.max(g, axis=-1, keepdims=True)
        idx = jnp.min(jnp.where(g == m, lane, LANE), axis=-1, keepdims=True)
        hit = lane == idx
        sel = jnp.where(hit, 1.0, sel)
        g = jnp.where(hit, -jnp.inf, g)
    return sel


def _attend(q_ref, k32, v32, bias, o_ref, rep, scale):
    k = k32.astype(MXU_DTYPE)
    v = v32.astype(MXU_DTYPE)
    for r in range(rep):
        s = _dot_nt(q_ref[r], k) * scale + bias
        m = jnp.max(s, axis=-1, keepdims=True)
        p = jnp.exp(s - m)
        l = jnp.sum(p, axis=-1, keepdims=True)
        o = jnp.dot(p.astype(MXU_DTYPE), v, preferred_element_type=jnp.float32) / l
        o_ref[:, r * HEAD_DIM:(r + 1) * HEAD_DIM] = o.astype(o_ref.dtype)


def _dsa_prompt_kernel(q_ref, k_ref, v_ref, qi_ref, wq_ref, kiw_ref, o_ref, bias_ref, key_ref,
                       *, tq, T, rep, n_keep, scale):
    i = pl.program_id(1)
    g = pl.program_id(2)

    @pl.when(g == 0)
    def _():
        kiw = kiw_ref[...]
        klane = lax.broadcasted_iota(jnp.int32, (T, LANE), 1)
        ka32 = jnp.where(klane < IDX_DIM, kiw, 0.0)
        ka = ka32.astype(MXU_DTYPE)
        kb = pltpu.roll(ka32, IDX_DIM, 1).astype(MXU_DTYPE)
        wq = wq_ref[...]
        score = None
        for hp in range(IDX_HEADS // 2):
            qp = qi_ref[hp]
            c = (jnp.maximum(_dot_nt(qp, ka), 0.0) * wq[:, IDX_DIM + 2 * hp:IDX_DIM + 2 * hp + 1]
                 + jnp.maximum(_dot_nt(qp, kb), 0.0) * wq[:, IDX_DIM + 2 * hp + 1:IDX_DIM + 2 * hp + 2])
            score = c if score is None else score + c
        row = lax.broadcasted_iota(jnp.int32, (tq, T), 0) + i * tq
        col = lax.broadcasted_iota(jnp.int32, (tq, T), 1)
        causal = col <= row
        key_ref[...] = jnp.where(causal, _float_keys(score), INT_MIN)

        def count_ge(cand):
            return jnp.sum(jnp.where(key_ref[...] >= cand, 1, 0), axis=-1, keepdims=True)

        thr = _kth_largest(count_ge, (tq, 1), n_keep)
        keys = key_ref[...]
        gt = keys > thr
        eq = keys == thr
        n_gt = jnp.sum(jnp.where(gt, 1, 0), axis=-1, keepdims=True)
        n_eq = jnp.sum(jnp.where(eq, 1, 0), axis=-1, keepdims=True)
        bias_ref[...] = jnp.where((gt | eq) & causal, 0.0, NEG_BIAS)
        tie = (thr > INT_MIN) & (n_gt + n_eq > n_keep)

        @pl.when(jnp.max(jnp.where(tie, 1, 0)) > 0)
        def _():
            need = n_keep - n_gt
            nbits = T.bit_length()

            def body(it, lim):
                cand = lim + lax.shift_left(jnp.int32(1), nbits - 1 - it)
                c = jnp.sum(jnp.where(eq & (col < cand), 1, 0), axis=-1, keepdims=True)
                return jnp.where(c <= need, cand, lim)

            lim = lax.fori_loop(0, nbits, body, jnp.zeros((tq, 1), jnp.int32))
            bias_ref[...] = jnp.where((gt | (eq & (col < lim))) & causal, 0.0, NEG_BIAS)

    _attend(q_ref, k_ref[...], v_ref[...], bias_ref[...], o_ref, rep, scale)


def _dsa_prompt(q128, k128, v128, qi, kiwi, B, T, rep):
    tq = min(256, T)
    nT = T // tq
    n_keep = min(DSA_TOPK, T // 4)
    kern = functools.partial(_dsa_prompt_kernel, tq=tq, T=T, rep=rep, n_keep=n_keep,
                             scale=HEAD_DIM ** -0.5)
    return pl.pallas_call(
        kern, grid=(B, nT, KV_A),
        in_specs=[
            pl.BlockSpec((rep, tq, HEAD_DIM), lambda b, i, g: (g, b * nT + i, 0)),
            pl.BlockSpec((T, HEAD_DIM), lambda b, i, g: (b, g)),
            pl.BlockSpec((T, HEAD_DIM), lambda b, i, g: (b, g)),
            pl.BlockSpec((IDX_HEADS // 2, tq, LANE), lambda b, i, g: (0, b * nT + i, 0)),
            pl.BlockSpec((tq, LANE), lambda b, i, g: (b * nT + i, 0)),
            pl.BlockSpec((T, LANE), lambda b, i, g: (b, 0)),
        ],
        out_specs=pl.BlockSpec((tq, rep * HEAD_DIM), lambda b, i, g: (b * nT + i, g)),
        out_shape=jax.ShapeDtypeStruct((B * T, KV_A * rep * HEAD_DIM), MXU_DTYPE),
        scratch_shapes=[pltpu.VMEM((tq, T), jnp.float32), pltpu.VMEM((tq, T), jnp.int32)],
        compiler_params=_cparams("parallel", "parallel", "arbitrary"), name="dsa_prompt",
    )(q128, k128, v128, qi, kiwi, kiwi)


def _moba_prompt_kernel(q_ref, k_ref, v_ref, o_ref, bias_ref, *, bs, nblk, rep, n_pick, scale):
    i = pl.program_id(1)
    kf = k_ref[...]
    qsum = q_ref[0].astype(jnp.float32)
    for r in range(1, rep):
        qsum = qsum + q_ref[r].astype(jnp.float32)
    lane = lax.broadcasted_iota(jnp.int32, (bs, LANE), 1)
    gate = jnp.full((bs, LANE), -jnp.inf, jnp.float32)
    for n in range(nblk):
        mean_n = jnp.mean(kf[n * bs:(n + 1) * bs], axis=0, keepdims=True)
        gate = jnp.where(lane == n, jnp.sum(qsum * mean_n, axis=-1, keepdims=True), gate)
    past = lane < i
    sel = jnp.where(past, _topk_lanes(jnp.where(past, gate, -jnp.inf), n_pick), 0.0)
    row = lax.broadcasted_iota(jnp.int32, (bs, bs), 0)
    col = lax.broadcasted_iota(jnp.int32, (bs, bs), 1)
    own_bias = jnp.where(col <= row, 0.0, NEG_BIAS)
    for n in range(nblk):
        picked = jnp.where(sel[:, n:n + 1] > 0.5, 0.0, NEG_BIAS)
        bias_ref[:, n * bs:(n + 1) * bs] = jnp.where(i == n, own_bias, jnp.broadcast_to(picked, (bs, bs)))
    _attend(q_ref, kf, v_ref[...], bias_ref[...], o_ref, rep, scale)


def _moba_prompt(q128, k128, v128, B, T, rep, h_a):
    bs = MOBA_BLOCK
    assert T % bs == 0
    nblk = T // bs
    assert 1 <= nblk <= LANE
    kern = functools.partial(_moba_prompt_kernel, bs=bs, nblk=nblk, rep=rep,
                             n_pick=min(MOBA_TOPK, nblk), scale=HEAD_DIM ** -0.5)
    return pl.pallas_call(
        kern, grid=(B, nblk, KV_B),
        in_specs=[
            pl.BlockSpec((rep, bs, HEAD_DIM), lambda b, i, g: (h_a // rep + g, b * nblk + i, 0)),
            pl.BlockSpec((T, HEAD_DIM), lambda b, i, g: (b, KV_A + g)),
            pl.BlockSpec((T, HEAD_DIM), lambda b, i, g: (b, KV_A + g)),
        ],
        out_specs=pl.BlockSpec((bs, rep * HEAD_DIM), lambda b, i, g: (b * nblk + i, g)),
        out_shape=jax.ShapeDtypeStruct((B * T, KV_B * rep * HEAD_DIM), MXU_DTYPE),
        scratch_shapes=[pltpu.VMEM((bs, T), jnp.float32)],
        compiler_params=_cparams("parallel", "parallel", "parallel"), name="moba_prompt",
    )(q128, k128, v128)


def _page_copies(pt_ref, pool_ref, layer, buf, sem, b, chunk, slot, pages, page):
    out = []
    for p in range(pages):
        phys = pt_ref[b, chunk * pages + p]
        out.append(pltpu.make_async_copy(pool_ref.at[layer, phys], buf.at[slot, pl.ds(p * page, page)],
                                         sem.at[slot]))
    return out


def _paged_step(c, nchunk, start_fn, wait_fn):
    @pl.when(c == 0)
    def _():
        start_fn(0, 0)

    slot = c % 2
    wait_fn(c, slot)

    @pl.when(c + 1 < nchunk)
    def _():
        start_fn(c + 1, 1 - slot)

    return slot


def _dsa_decode_mask_kernel(pt_ref, qi_ref, wc_ref, pool_ref, knew_ref, bp_ref, bn_ref,
                            kbuf, sem, sp_ref, sn_ref, *, layer, nchunk, pages, page, Td, n_keep, past):
    b = pl.program_id(0)
    c = pl.program_id(1)
    CH = pages * page

    def start_fn(cc, slot):
        for cp in _page_copies(pt_ref, pool_ref, layer, kbuf, sem, b, cc, slot, pages, page):
            cp.start()

    def wait_fn(cc, slot):
        for cp in _page_copies(pt_ref, pool_ref, layer, kbuf, sem, b, cc, slot, pages, page):
            cp.wait()

    def scores(kc, width):
        d = _dot_nt(qi_ref[0], kc.astype(MXU_DTYPE))
        r = jnp.maximum(d, 0.0) * wc_ref[0]
        return jnp.sum(r.reshape(IDX_HEADS, Td, width), axis=0)

    @pl.when(c < nchunk)
    def _():
        slot = _paged_step(c, nchunk, start_fn, wait_fn)
        sp_ref[c] = scores(kbuf[slot], CH)

    @pl.when(c == nchunk)
    def _():
        sn_ref[...] = scores(knew_ref[0], LANE)
        trow = lax.broadcasted_iota(jnp.int32, (Td, LANE), 0)
        ncol = lax.broadcasted_iota(jnp.int32, (Td, LANE), 1)
        new_ok = ncol <= trow
        kp = _float_keys(sp_ref[...])
        kn = jnp.where(new_ok, _float_keys(sn_ref[...]), INT_MIN)

        def count(mp, mn):
            return (jnp.sum(jnp.sum(jnp.where(mp, 1, 0), axis=0), axis=-1, keepdims=True)
                    + jnp.sum(jnp.where(mn, 1, 0), axis=-1, keepdims=True))

        thr = _kth_largest(lambda cand: count(kp >= cand[None], kn >= cand), (Td, 1), n_keep)
        gt_p, eq_p = kp > thr[None], kp == thr[None]
        gt_n, eq_n = kn > thr, kn == thr
        n_gt = count(gt_p, gt_n)
        n_eq = count(eq_p, eq_n)
        bp_ref[0] = jnp.where(gt_p | eq_p, 0.0, NEG_BIAS)
        bn_ref[0] = jnp.where((gt_n | eq_n) & new_ok, 0.0, NEG_BIAS)
        tie = (thr > INT_MIN) & (n_gt + n_eq > n_keep)

        @pl.when(jnp.max(jnp.where(tie, 1, 0)) > 0)
        def _():
            need = n_keep - n_gt
            pos_p = (lax.broadcasted_iota(jnp.int32, (nchunk, Td, CH), 0) * CH
                     + lax.broadcasted_iota(jnp.int32, (nchunk, Td, CH), 2))
            pos_n = past + ncol
            nbits = (past + LANE).bit_length()

            def body(it, lim):
                cand = lim + lax.shift_left(jnp.int32(1), nbits - 1 - it)
                cnt = count(eq_p & (pos_p < cand[None]), eq_n & (pos_n < cand))
                return jnp.where(cnt <= need, cand, lim)

            lim = lax.fori_loop(0, nbits, body, jnp.zeros((Td, 1), jnp.int32))
            bp_ref[0] = jnp.where(gt_p | (eq_p & (pos_p < lim[None])), 0.0, NEG_BIAS)
            bn_ref[0] = jnp.where((gt_n | (eq_n & (pos_n < lim))) & new_ok, 0.0, NEG_BIAS)


def _dsa_decode_mask(page_table, qi_s, wcol, pool, knew, layer, Td):
    Bd, n_pages = page_table.shape
    page = pool.shape[2]
    pages = min(PAGES_PER_CHUNK, n_pages)
    assert n_pages % pages == 0
    nchunk = n_pages // pages
    CH = pages * page
    past = n_pages * page
    n_keep = min(DSA_TOPK, (past + Td) // 4)
    HT = qi_s.shape[1]
    kern = functools.partial(_dsa_decode_mask_kernel, layer=layer, nchunk=nchunk, pages=pages, page=page,
                             Td=Td, n_keep=n_keep, past=past)
    grid_spec = pltpu.PrefetchScalarGridSpec(
        num_scalar_prefetch=1, grid=(Bd, nchunk + 1),
        in_specs=[
            pl.BlockSpec((1, HT, IDX_DIM), lambda b, c, pt: (b, 0, 0)),
            pl.BlockSpec((1, HT, 1), lambda b, c, pt: (b, 0, 0)),
            pl.BlockSpec(memory_space=pl.ANY),
            pl.BlockSpec((1, LANE, IDX_DIM), lambda b, c, pt: (b, 0, 0)),
        ],
        out_specs=[
            pl.BlockSpec((1, nchunk, Td, CH), lambda b, c, pt: (b, 0, 0, 0)),
            pl.BlockSpec((1, Td, LANE), lambda b, c, pt: (b, 0, 0)),
        ],
        scratch_shapes=[
            pltpu.VMEM((2, CH, IDX_DIM), jnp.float32),
            pltpu.SemaphoreType.DMA((2,)),
            pltpu.VMEM((nchunk, Td, CH), jnp.float32),
            pltpu.VMEM((Td, LANE), jnp.float32),
        ])
    return pl.pallas_call(
        kern, grid_spec=grid_spec,
        out_shape=[jax.ShapeDtypeStruct((Bd, nchunk, Td, CH), jnp.float32),
                   jax.ShapeDtypeStruct((Bd, Td, LANE), jnp.float32)],
        compiler_params=_cparams("arbitrary", "arbitrary"), name="dsa_decode_mask",
    )(page_table, qi_s, wcol, pool, knew)


def _moba_decode_mask_kernel(pt_ref, q_ref, pool_ref, bp_ref, kbuf, sem, gate_ref,
                             *, layer, nchunk, pages, page, Td, rep, bs, n_pick):
    b = pl.program_id(0)
    c = pl.program_id(1)
    CH = pages * page
    per = CH // bs
    nblk = nchunk * per

    def start_fn(cc, slot):
        for cp in _page_copies(pt_ref, pool_ref, layer, kbuf, sem, b, cc, slot, pages, page):
            cp.start()

    def wait_fn(cc, slot):
        for cp in _page_copies(pt_ref, pool_ref, layer, kbuf, sem, b, cc, slot, pages, page):
            cp.wait()

    @pl.when(c == 0)
    def _():
        gate_ref[...] = jnp.full(gate_ref.shape, -jnp.inf, jnp.float32)

    slot = _paged_step(c, nchunk, start_fn, wait_fn)
    lane = lax.broadcasted_iota(jnp.int32, (Td, LANE), 1)
    qsum = [jnp.sum(q_ref[0, g].astype(jnp.float32).reshape(rep, Td, HEAD_DIM), axis=0) for g in range(KV_B)]
    for nb in range(per):
        mean_nb = jnp.mean(kbuf[slot, nb * bs:(nb + 1) * bs, :], axis=0, keepdims=True)
        for g in range(KV_B):
            gn = jnp.sum(qsum[g] * mean_nb[:, g * HEAD_DIM:(g + 1) * HEAD_DIM], axis=-1, keepdims=True)
            gate_ref[g] = jnp.where(lane == c * per + nb, gn, gate_ref[g])

    @pl.when(c == nchunk - 1)
    def _():
        for g in range(KV_B):
            sel = jnp.where(lane < nblk, _topk_lanes(gate_ref[g], n_pick), 0.0)
            for n in range(nblk):
                picked = jnp.where(sel[:, n:n + 1] > 0.5, 0.0, NEG_BIAS)
                bp_ref[0, n // per, g, :, (n % per) * bs:(n % per + 1) * bs] = jnp.broadcast_to(picked, (Td, bs))


def _moba_decode_mask(page_table, q_s, pool, layer, Td, rep):
    Bd, n_pages = page_table.shape
    page = pool.shape[2]
    pages = min(PAGES_PER_CHUNK, n_pages)
    assert n_pages % pages == 0
    nchunk = n_pages // pages
    CH = pages * page
    bs = MOBA_BLOCK
    assert CH % bs == 0 and Td <= bs
    nblk = n_pages * page // bs
    assert nblk <= LANE
    kern = functools.partial(_moba_decode_mask_kernel, layer=layer, nchunk=nchunk, pages=pages, page=page,
                             Td=Td, rep=rep, bs=bs, n_pick=min(MOBA_TOPK, (n_pages * page + Td) // bs))
    grid_spec = pltpu.PrefetchScalarGridSpec(
        num_scalar_prefetch=1, grid=(Bd, nchunk),
        in_specs=[
            pl.BlockSpec((1, KV_B, rep * Td, HEAD_DIM), lambda b, c, pt: (b, 0, 0, 0)),
            pl.BlockSpec(memory_space=pl.ANY),
        ],
        out_specs=pl.BlockSpec((1, nchunk, KV_B, Td, CH), lambda b, c, pt: (b, 0, 0, 0, 0)),
        scratch_shapes=[
            pltpu.VMEM((2, CH, KV_B * HEAD_DIM), jnp.float32),
            pltpu.SemaphoreType.DMA((2,)),
            pltpu.VMEM((KV_B, Td, LANE), jnp.float32),
        ])
    return pl.pallas_call(
        kern, grid_spec=grid_spec,
        out_shape=jax.ShapeDtypeStruct((Bd, nchunk, KV_B, Td, CH), jnp.float32),
        compiler_params=_cparams("arbitrary", "arbitrary"), name="moba_decode_mask",
    )(page_table, q_s, pool)


def _paged_attn_kernel(pt_ref, q_ref, kpool_ref, vpool_ref, knew_ref, vnew_ref, bp_ref, bn_ref, o_ref,
                       kbuf, vbuf, sem, m_ref, l_ref, acc_ref,
                       *, layer, nchunk, pages, page, Td, rep, n_kv, per_group_bias, scale):
    b = pl.program_id(0)
    c = pl.program_id(1)

    def start_fn(cc, slot):
        for cp in (_page_copies(pt_ref, kpool_ref, layer, kbuf, sem.at[0], b, cc, slot, pages, page)
                   + _page_copies(pt_ref, vpool_ref, layer, vbuf, sem.at[1], b, cc, slot, pages, page)):
            cp.start()

    def wait_fn(cc, slot):
        for cp in (_page_copies(pt_ref, kpool_ref, layer, kbuf, sem.at[0], b, cc, slot, pages, page)
                   + _page_copies(pt_ref, vpool_ref, layer, vbuf, sem.at[1], b, cc, slot, pages, page)):
            cp.wait()

    @pl.when(c == 0)
    def _():
        m_ref[...] = jnp.full(m_ref.shape, -jnp.inf, jnp.float32)
        l_ref[...] = jnp.zeros(l_ref.shape, jnp.float32)
        acc_ref[...] = jnp.zeros(acc_ref.shape, jnp.float32)

    def process(kc, vc, bias_fn):
        for g in range(n_kv):
            k = kc[:, g * HEAD_DIM:(g + 1) * HEAD_DIM].astype(MXU_DTYPE)
            v = vc[:, g * HEAD_DIM:(g + 1) * HEAD_DIM].astype(MXU_DTYPE)
            bias = bias_fn(g)
            s = _dot_nt(q_ref[0, g], k) * scale + jnp.concatenate([bias] * rep, axis=0)
            m_old = m_ref[g]
            m_new = jnp.maximum(m_old, jnp.max(s, axis=-1, keepdims=True))
            a = jnp.exp(m_old - m_new)
            p = jnp.exp(s - m_new)
            l_ref[g] = a * l_ref[g] + jnp.sum(p, axis=-1, keepdims=True)
            acc_ref[g] = a * acc_ref[g] + jnp.dot(p.astype(MXU_DTYPE), v, preferred_element_type=jnp.float32)
            m_ref[g] = m_new

    @pl.when(c < nchunk)
    def _():
        slot = _paged_step(c, nchunk, start_fn, wait_fn)
        process(kbuf[slot], vbuf[slot], lambda g: bp_ref[0, 0, g if per_group_bias else 0])

    @pl.when(c == nchunk)
    def _():
        process(knew_ref[0], vnew_ref[0], lambda g: bn_ref[0])
        for g in range(n_kv):
            o_ref[0, g] = acc_ref[g] / l_ref[g]


def _paged_attention(page_table, q_s, kpool, vpool, knew, vnew, bias_past, bias_new, layer, Td, rep):
    Bd, n_pages = page_table.shape
    page = kpool.shape[2]
    pages = min(PAGES_PER_CHUNK, n_pages)
    nchunk = n_pages // pages
    CH = pages * page
    n_kv = q_s.shape[1]
    R = rep * Td
    W = n_kv * HEAD_DIM
    gb = bias_past.shape[2]
    nb_new = bias_new.shape[0]
    kern = functools.partial(_paged_attn_kernel, layer=layer, nchunk=nchunk, pages=pages, page=page, Td=Td,
                             rep=rep, n_kv=n_kv, per_group_bias=gb > 1, scale=HEAD_DIM ** -0.5)
    grid_spec = pltpu.PrefetchScalarGridSpec(
        num_scalar_prefetch=1, grid=(Bd, nchunk + 1),
        in_specs=[
            pl.BlockSpec((1, n_kv, R, HEAD_DIM), lambda b, c, pt: (b, 0, 0, 0)),
            pl.BlockSpec(memory_space=pl.ANY),
            pl.BlockSpec(memory_space=pl.ANY),
            pl.BlockSpec((1, LANE, W), lambda b, c, pt: (b, 0, 0)),
            pl.BlockSpec((1, LANE, W), lambda b, c, pt: (b, 0, 0)),
            pl.BlockSpec((1, 1, gb, Td, CH), lambda b, c, pt: (b, jnp.minimum(c, nchunk - 1), 0, 0, 0)),
            pl.BlockSpec((1, Td, LANE), lambda b, c, pt: (b if nb_new > 1 else 0, 0, 0)),
        ],
        out_specs=pl.BlockSpec((1, n_kv, R, HEAD_DIM), lambda b, c, pt: (b, 0, 0, 0)),
        scratch_shapes=[
            pltpu.VMEM((2, CH, W), jnp.float32),
            pltpu.VMEM((2, CH, W), jnp.float32),
            pltpu.SemaphoreType.DMA((2, 2)),
            pltpu.VMEM((n_kv, R, 1), jnp.float32),
            pltpu.VMEM((n_kv, R, 1), jnp.float32),
            pltpu.VMEM((n_kv, R, HEAD_DIM), jnp.float32),
        ])
    return pl.pallas_call(
        kern, grid_spec=grid_spec,
        out_shape=jax.ShapeDtypeStruct((Bd, n_kv, R, HEAD_DIM), jnp.float32),
        compiler_params=_cparams("arbitrary", "arbitrary"), name="paged_attention",
    )(page_table, q_s, kpool, vpool, knew, vnew, bias_past, bias_new)


def _route_kernel(lg_ref, eid_ref, gate_ref):
    lg = lg_ref[...]
    lane = lax.broadcasted_iota(jnp.int32, lg.shape, 1)
    gmask = lane < N_GROUPS
    gl = jnp.where(gmask, lg, -jnp.inf)
    gmax = jnp.max(gl, axis=-1, keepdims=True)
    gsel = jnp.min(jnp.where(gl == gmax, lane, LANE), axis=-1, keepdims=True)
    g_w = 1.0 / jnp.sum(jnp.where(gmask, jnp.exp(gl - gmax), 0.0), axis=-1, keepdims=True)
    lo = N_GROUPS + gsel * EXPERTS_PER_GROUP
    el = jnp.where((lane >= lo) & (lane < lo + EXPERTS_PER_GROUP), lg, -jnp.inf)
    v1 = jnp.max(el, axis=-1, keepdims=True)
    i1 = jnp.min(jnp.where(el == v1, lane, LANE), axis=-1, keepdims=True)
    el2 = jnp.where(lane == i1, -jnp.inf, el)
    v2 = jnp.max(el2, axis=-1, keepdims=True)
    i2 = jnp.min(jnp.where(el2 == v2, lane, LANE), axis=-1, keepdims=True)
    e2 = jnp.exp(v2 - v1)
    p1 = 1.0 / (1.0 + e2)
    eid_ref[...] = jnp.where(lane == 0, i1 - N_GROUPS, jnp.where(lane == 1, i2 - N_GROUPS, 0))
    gate_ref[...] = jnp.where(lane == 0, g_w * p1, jnp.where(lane == 1, g_w * (e2 * p1), 0.0))


def _route(logits):
    Np = logits.shape[0]
    tr = _pick_tile(Np, 1024, 8)
    spec = pl.BlockSpec((tr, LANE), lambda i: (i, 0))
    return pl.pallas_call(
        _route_kernel, grid=(Np // tr,), in_specs=[spec], out_specs=[spec, spec],
        out_shape=[jax.ShapeDtypeStruct((Np, LANE), jnp.int32), jax.ShapeDtypeStruct((Np, LANE), jnp.float32)],
        compiler_params=_cparams("parallel"), name="route")(logits)


def _moe_plan(eid, n_experts, rc, nch_max):
    M = eid.shape[0]
    C = rc * nch_max
    order = jnp.argsort(eid, stable=True).astype(jnp.int32)
    eid_s = eid[order]
    counts = jnp.bincount(eid, length=n_experts).astype(jnp.int32)
    start = jnp.cumsum(counts) - counts
    pcounts = (counts + rc - 1) // rc * rc
    pstart = jnp.cumsum(pcounts) - pcounts
    slot_s = (pstart[eid_s] + jnp.arange(M, dtype=jnp.int32) - start[eid_s]).astype(jnp.int32)
    R = _round_up(M, rc) + n_experts * rc
    tok = jnp.zeros((R,), jnp.int32).at[slot_s].set(order // EXPERT_TOPK)
    dest = jnp.zeros((M,), jnp.int32).at[order].set(slot_s)
    nseg = (pcounts + C - 1) // C
    send = jnp.cumsum(nseg)
    sstart = send - nseg
    n_seg = -(-R // C) + n_experts
    sidx = jnp.arange(n_seg, dtype=jnp.int32)
    e_of = jnp.minimum(jnp.searchsorted(send, sidx, side="right"), n_experts - 1).astype(jnp.int32)
    active = sidx < send[-1]
    local = sidx - sstart[e_of]
    nch = jnp.where(active, jnp.clip((pcounts[e_of] - local * C + rc - 1) // rc, 0, nch_max), 0)
    used = pstart[-1] + pcounts[-1]
    idle_row0 = used + (sidx - send[-1]) * C
    row0 = jnp.where(active, pstart[e_of] + local * C, jnp.minimum(idle_row0, R))
    nzero = jnp.where(active, 0, jnp.clip((R - idle_row0) // rc, 0, nch_max))
    seg_e = jnp.where(active, e_of, e_of[jnp.maximum(send[-1] - 1, 0)])
    i32 = jnp.int32
    return tok, dest, seg_e.astype(i32), row0.astype(i32), nch.astype(i32), nzero.astype(i32), R


def _moe_kernel(seg_e_ref, row0_ref, nch_ref, nzero_ref, tok_ref, x_hbm, wg_ref, wu_ref, wd_ref, y_hbm,
                stage, xbuf, acc, wgb, wub, wdb, gsem, osem, *, rc, nf):
    del seg_e_ref
    s = pl.program_id(0)
    f = pl.program_id(1)
    nch = nch_ref[s]
    row0 = row0_ref[s]
    nzero = nzero_ref[s]

    @pl.when((f == 0) & (nzero > 0))
    def _():
        acc[pl.ds(0, rc), :] = jnp.zeros((rc, acc.shape[1]), jnp.float32)

        def copy(ch):
            dst = y_hbm.at[pl.ds(pl.multiple_of(row0 + ch * rc, rc), rc)]
            return pltpu.make_async_copy(acc.at[pl.ds(0, rc)], dst, osem)

        def start(ch, carry):
            copy(ch).start()
            return carry

        def wait(ch, carry):
            copy(ch).wait()
            return carry
        lax.fori_loop(0, nzero, start, 0)
        lax.fori_loop(0, nzero, wait, 0)

    def gather(ch, slot, wait):
        def body(r, carry):
            tok = tok_ref[row0 + ch * rc + r]
            cp = pltpu.make_async_copy(x_hbm.at[pl.ds(tok, 1)], stage.at[slot, pl.ds(r, 1)], gsem.at[slot])
            if wait:
                cp.wait()
            else:
                cp.start()
            return carry
        lax.fori_loop(0, rc, body, 0)

    @pl.when((f == 0) & (nch > 0))
    def _():
        gather(0, 0, False)

        def body(ch, carry):
            slot = ch % 2
            gather(ch, slot, True)

            @pl.when(ch + 1 < nch)
            def _():
                gather(ch + 1, 1 - slot, False)

            rows = pl.ds(pl.multiple_of(ch * rc, rc), rc)
            xbuf[rows, :] = stage[slot].astype(xbuf.dtype)
            acc[rows, :] = jnp.zeros((rc, acc.shape[1]), jnp.float32)
            return carry
        lax.fori_loop(0, nch, body, 0)

    @pl.when(nch > 0)
    def _():
        wgb[...] = wg_ref[...].astype(wgb.dtype)
        wub[...] = wu_ref[...].astype(wub.dtype)
        wdb[...] = wd_ref[...].astype(wdb.dtype)

        def body(ch, carry):
            rows = pl.ds(pl.multiple_of(ch * rc, rc), rc)
            x = xbuf[rows, :]
            a = jnp.dot(x, wgb[...], preferred_element_type=jnp.float32)
            u = jnp.dot(x, wub[...], preferred_element_type=jnp.float32)
            h = (a * jax.nn.sigmoid(a) * u).astype(wdb.dtype)
            acc[rows, :] += jnp.dot(h, wdb[...], preferred_element_type=jnp.float32)
            return carry
        lax.fori_loop(0, nch, body, 0)

    @pl.when((f == nf - 1) & (nch > 0))
    def _():
        def copy(ch):
            rows = pl.ds(pl.multiple_of(ch * rc, rc), rc)
            return pltpu.make_async_copy(acc.at[rows], y_hbm.at[pl.ds(pl.multiple_of(row0 + ch * rc, rc), rc)], osem)

        def start(ch, carry):
            copy(ch).start()
            return carry

        def wait(ch, carry):
            copy(ch).wait()
            return carry
        lax.fori_loop(0, nch, start, 0)
        lax.fori_loop(0, nch, wait, 0)


def _moe_experts(x32, plan, w_gate, w_up, w_down, layer):
    tok, _, seg_e, row0, nch, nzero, R = plan
    Np, D = x32.shape
    F = w_gate.shape[-1]
    ft = min(MOE_FT, F)
    nf = F // ft
    rc = MOE_ROWS
    C = rc * MOE_CHUNKS
    n_seg = seg_e.shape[0]

    def f_eff(s, f, nch_ref):
        return jnp.where(nch_ref[s] > 0, f, nf - 1)

    grid_spec = pltpu.PrefetchScalarGridSpec(
        num_scalar_prefetch=5, grid=(n_seg, nf),
        in_specs=[
            pl.BlockSpec(memory_space=pl.ANY),
            pl.BlockSpec((None, None, D, ft), lambda s, f, se, r0, nc, nz, tk: (layer, se[s], 0, f_eff(s, f, nc))),
            pl.BlockSpec((None, None, D, ft), lambda s, f, se, r0, nc, nz, tk: (layer, se[s], 0, f_eff(s, f, nc))),
            pl.BlockSpec((None, None, ft, D), lambda s, f, se, r0, nc, nz, tk: (layer, se[s], f_eff(s, f, nc), 0)),
        ],
        out_specs=pl.BlockSpec(memory_space=pl.ANY),
        scratch_shapes=[
            pltpu.VMEM((2, rc, D), jnp.float32),
            pltpu.VMEM((C, D), MXU_DTYPE),
            pltpu.VMEM((C, D), jnp.float32),
            pltpu.VMEM((D, ft), MXU_DTYPE),
            pltpu.VMEM((D, ft), MXU_DTYPE),
            pltpu.VMEM((ft, D), MXU_DTYPE),
            pltpu.SemaphoreType.DMA((2,)),
            pltpu.SemaphoreType.DMA(()),
        ])
    return pl.pallas_call(
        functools.partial(_moe_kernel, rc=rc, nf=nf), grid_spec=grid_spec,
        out_shape=jax.ShapeDtypeStruct((R, D), jnp.float32),
        compiler_params=_cparams("arbitrary", "arbitrary"), name="moe_experts",
    )(seg_e, row0, nch, nzero, tok, x32, w_gate, w_up, w_down)


def _combine_kernel(dest_ref, x_ref, gate_ref, gam_ref, bet_ref, y_hbm, o32_ref, o16_ref, ybuf, sem,
                    *, tc, alpha):
    i = pl.program_id(0)

    def rows(wait):
        def body(r, carry):
            for k in range(EXPERT_TOPK):
                slot = dest_ref[(i * tc + r) * EXPERT_TOPK + k]
                cp = pltpu.make_async_copy(y_hbm.at[pl.ds(slot, 1)], ybuf.at[k, pl.ds(r, 1)], sem)
                if wait:
                    cp.wait()
                else:
                    cp.start()
            return carry
        lax.fori_loop(0, tc, body, 0)

    rows(False)
    rows(True)
    gate = gate_ref[...]
    ffn = gate[:, 0:1] * ybuf[0]
    for k in range(1, EXPERT_TOPK):
        ffn = ffn + gate[:, k:k + 1] * ybuf[k]
    v = alpha * x_ref[...] + ffn
    mu = jnp.mean(v, axis=-1, keepdims=True)
    d = v - mu
    var = jnp.mean(d * d, axis=-1, keepdims=True)
    y = d * lax.rsqrt(var + LN_EPS) * gam_ref[...] + bet_ref[...]
    o32_ref[...] = y
    o16_ref[...] = y.astype(o16_ref.dtype)


def _moe_combine(dest, x32, gates, y_sorted, gam, bet, alpha):
    Np, D = x32.shape
    tc = _pick_tile(Np, 192, 16)
    row = lambda i, d: (i, 0)
    grid_spec = pltpu.PrefetchScalarGridSpec(
        num_scalar_prefetch=1, grid=(Np // tc,),
        in_specs=[
            pl.BlockSpec((tc, D), row),
            pl.BlockSpec((tc, LANE), row),
            pl.BlockSpec((1, D), lambda i, d: (0, 0)),
            pl.BlockSpec((1, D), lambda i, d: (0, 0)),
            pl.BlockSpec(memory_space=pl.ANY),
        ],
        out_specs=[pl.BlockSpec((tc, D), row), pl.BlockSpec((tc, D), row)],
        scratch_shapes=[pltpu.VMEM((EXPERT_TOPK, tc, D), jnp.float32), pltpu.SemaphoreType.DMA(())])
    return pl.pallas_call(
        functools.partial(_combine_kernel, tc=tc, alpha=alpha), grid_spec=grid_spec,
        out_shape=[jax.ShapeDtypeStruct((Np, D), jnp.float32), jax.ShapeDtypeStruct((Np, D), MXU_DTYPE)],
        compiler_params=_cparams("arbitrary"), name="moe_combine",
    )(dest, x32, gates, gam.reshape(1, D), bet.reshape(1, D), y_sorted)


def _rope_tables(pos, scale_wi):
    def cs(dim):
        half = dim // 2
        inv = ROPE_THETA ** (-jnp.arange(half, dtype=jnp.float32) * 2.0 / dim)
        ang = pos.astype(jnp.float32)[:, None] * inv[None, :]
        return jnp.cos(ang), jnp.sin(ang)
    c, s = cs(HEAD_DIM)
    t128 = (jnp.concatenate([c, c], -1), jnp.concatenate([-s, s], -1))
    c, s = cs(IDX_DIM)
    c64 = jnp.concatenate([c, c], -1)
    s64 = jnp.concatenate([-s, s], -1)
    t64 = (jnp.tile(c64, (1, LANE // IDX_DIM)), jnp.tile(s64, (1, LANE // IDX_DIM)))
    n = pos.shape[0]
    pad = LANE - IDX_DIM - IDX_HEADS
    tkw = (jnp.concatenate([c64, jnp.full((n, IDX_HEADS), scale_wi, jnp.float32), jnp.zeros((n, pad), jnp.float32)], -1),
           jnp.concatenate([s64, jnp.zeros((n, LANE - IDX_DIM), jnp.float32)], -1))
    return t128, t64, tkw


def kernel(x_prompt, x_sample, cache_a_k, cache_a_v, cache_a_kidx, cache_b_k, cache_b_v, page_table, p_prompt, p_sample, ln_emb_g, ln_emb_b, w_in, w_out, ln1_g, ln1_b, w_route_group, w_route_expert, w_exp_gate, w_exp_up, w_exp_down, ln2_g, ln2_b, w_ple, w_ple_gate, ln3_g, ln3_b):
    B, T, D = x_prompt.shape
    Bd, Td, _ = x_sample.shape
    depth = w_in.shape[0]
    n_phys, page = cache_a_k.shape[1], cache_a_k.shape[2]
    past = page_table.shape[1] * page
    n_experts = w_exp_gate.shape[1]
    h_a = D // 2 // HEAD_DIM
    h_b = h_a
    rep_a, rep_b = h_a // KV_A, h_b // KV_B
    assert IDX_DIM * 2 == LANE and IDX_DIM + IDX_HEADS <= LANE and page == LANE
    assert N_GROUPS * (1 + EXPERTS_PER_GROUP) <= LANE and n_experts == N_GROUPS * EXPERTS_PER_GROUP
    alpha = (2.0 * depth) ** 0.25
    n_p, n_s = B * T, Bd * Td
    N = n_p + n_s
    Np = _round_up(N, ROW_ALIGN)
    f32 = jnp.float32

    def stream(a_p, a_s):
        w = a_p.shape[-1]
        return jnp.concatenate([a_p.reshape(n_p, w), a_s.reshape(n_s, w), jnp.zeros((Np - N, w), a_p.dtype)], 0)

    pos = jnp.concatenate([jnp.tile(jnp.arange(T, dtype=jnp.int32), B),
                           jnp.tile(past + jnp.arange(Td, dtype=jnp.int32), Bd),
                           jnp.zeros((Np - N,), jnp.int32)])
    t128, t64, tkw = _rope_tables(pos, IDX_HEADS ** -0.5 * IDX_DIM ** -0.5)

    widths = (h_a * HEAD_DIM, KV_A * HEAD_DIM, KV_A * HEAD_DIM, IDX_HEADS * IDX_DIM, IDX_DIM, IDX_HEADS,
              h_b * HEAD_DIM, KV_B * HEAD_DIM, KV_B * HEAD_DIM)
    offs = [0]
    for w in widths:
        offs.append(offs[-1] + w)
    assert offs[-1] == w_in.shape[2]

    def cols(w, *ids):
        return jnp.concatenate([w[:, offs[i]:offs[i + 1]] for i in ids], axis=1).astype(MXU_DTYPE)

    kv_w = KV_A * HEAD_DIM
    pool_ak = cache_a_k.reshape(depth, n_phys, page, kv_w)
    pool_av = cache_a_v.reshape(depth, n_phys, page, kv_w)
    pool_bk = cache_b_k.reshape(depth, n_phys, page, KV_B * HEAD_DIM)
    pool_bv = cache_b_v.reshape(depth, n_phys, page, KV_B * HEAD_DIM)
    trow = jnp.arange(Td, dtype=jnp.int32)[:, None]
    own_bias = jnp.where(jnp.arange(LANE, dtype=jnp.int32)[None, :] <= trow, 0.0, NEG_BIAS).astype(f32)[None]

    def fresh(rows):
        w = rows.shape[-1]
        return jnp.pad(rows.reshape(Bd, Td, w), ((0, 0), (0, LANE - Td), (0, 0)))

    def decode_q(q_sm, kv, rep):
        q = q_sm.reshape(kv, rep, Bd, Td, HEAD_DIM)
        return jnp.transpose(q, (2, 0, 1, 3, 4)).reshape(Bd, kv, rep * Td, HEAD_DIM)

    def decode_o(o, kv, rep):
        o = o.reshape(Bd, kv, rep, Td, HEAD_DIM)
        return jnp.transpose(o, (0, 3, 1, 2, 4)).reshape(n_s, kv * rep * HEAD_DIM)

    h32, h16 = _layer_norm([stream(x_prompt, x_sample)], ln_emb_g, ln_emb_b, "plain")
    rows_out = []
    for l in range(depth):
        wl = w_in[l]
        q128 = _matmul([h16], [cols(wl, 0, 6)], MXU_DTYPE, "rope128", t128, True, name="proj_q")
        k128 = _matmul([h16], [cols(wl, 1, 7)], f32, "rope128", t128, name="proj_k")
        v128 = _matmul([h16], [cols(wl, 2, 8)], f32, name="proj_v")
        qi = _matmul([h16], [cols(wl, 3)], MXU_DTYPE, "rope64", t64, True, name="proj_qi")
        w_kw = jnp.pad(cols(wl, 4, 5), ((0, 0), (0, LANE - IDX_DIM - IDX_HEADS)))
        kiwi = _matmul([h16], [w_kw], f32, "rope64", tkw, name="proj_kiwi")

        attn_a = _dsa_prompt(q128, k128, v128, qi, kiwi, B, T, rep_a)
        attn_b = _moba_prompt(q128, k128, v128, B, T, rep_b, h_a)

        ks, vs = k128[n_p:N], v128[n_p:N]
        kiwi_s = kiwi[n_p:N]
        qi_s = qi[:, n_p:N].reshape(IDX_HEADS // 2, Bd, Td, 2, IDX_DIM)
        qi_s = jnp.transpose(qi_s, (1, 0, 3, 2, 4)).reshape(Bd, IDX_HEADS * Td, IDX_DIM)
        wcol = jnp.transpose(kiwi_s[:, IDX_DIM:IDX_DIM + IDX_HEADS].reshape(Bd, Td, IDX_HEADS), (0, 2, 1))
        wcol = wcol.reshape(Bd, IDX_HEADS * Td, 1)
        bias_p, bias_n = _dsa_decode_mask(page_table, qi_s, wcol, cache_a_kidx, fresh(kiwi_s[:, :IDX_DIM]), l, Td)
        o_a = _paged_attention(page_table, decode_q(q128[:h_a, n_p:N], KV_A, rep_a), pool_ak, pool_av,
                               fresh(ks[:, :kv_w]), fresh(vs[:, :kv_w]), bias_p[:, :, None], bias_n, l, Td, rep_a)
        qb_s = decode_q(q128[h_a:, n_p:N], KV_B, rep_b)
        bias_b = _moba_decode_mask(page_table, qb_s, pool_bk, l, Td, rep_b)
        o_b = _paged_attention(page_table, qb_s, pool_bk, pool_bv, fresh(ks[:, kv_w:]), fresh(vs[:, kv_w:]),
                               bias_b, own_bias, l, Td, rep_b)
        tail = jnp.zeros((Np - N, h_a * HEAD_DIM), MXU_DTYPE)
        attn_a = jnp.concatenate([attn_a, decode_o(o_a, KV_A, rep_a).astype(MXU_DTYPE), tail], axis=0)
        attn_b = jnp.concatenate([attn_b, decode_o(o_b, KV_B, rep_b).astype(MXU_DTYPE), tail], axis=0)

        w_o = w_out[l].astype(MXU_DTYPE)
        mix = _matmul([attn_a, attn_b], [w_o[:h_a * HEAD_DIM], w_o[h_a * HEAD_DIM:]], f32, name="proj_out")
        w_r = jnp.pad(jnp.concatenate([w_route_group[l], w_route_expert[l]], axis=1),
                      ((0, 0), (0, LANE - N_GROUPS - n_experts)))
        w_rh = w_r.astype(MXU_DTYPE)
        w_rl = (w_r - w_rh.astype(f32)).astype(MXU_DTYPE)
        x32, x16, logits = _layer_norm([h32, mix], ln1_g[l], ln1_b[l], "add", alpha, (w_rh, w_rl))
        del x16

        eid, gates = _route(logits)
        plan = _moe_plan(eid[:N, :EXPERT_TOPK].reshape(-1), n_experts, MOE_ROWS, MOE_CHUNKS)
        y_sorted = _moe_experts(x32, plan, w_exp_gate, w_exp_up, w_exp_down, l)
        dest = jnp.pad(plan[1], (0, (Np - N) * EXPERT_TOPK))
        x32, x16 = _moe_combine(dest, x32, gates, y_sorted, ln2_g[l], ln2_b[l], alpha)

        gate_pre = _matmul([x16], [w_ple_gate[l].astype(MXU_DTYPE)], f32, name="ple_gate")
        p16 = stream(p_prompt[l], p_sample[l]).astype(MXU_DTYPE)
        ple = _matmul([p16], [w_ple[l].astype(MXU_DTYPE)], f32, name="ple_embed")
        h32, h16 = _layer_norm([x32, gate_pre, ple], ln3_g[l], ln3_b[l], "ple", alpha)
        rows_out.append((k128, v128, kiwi))

    def gather_rows(sel, lo, hi, lead):
        return jnp.stack([sel(r)[lo:hi].reshape(lead) for r in rows_out])

    outs = [h32[:n_p].reshape(B, T, D), h32[n_p:N].reshape(Bd, Td, D)]
    for lo, hi, lead in ((0, n_p, (B, T)), (n_p, N, (Bd, Td))):
        outs += [
            gather_rows(lambda r: r[0][:, :kv_w], lo, hi, lead + (KV_A, HEAD_DIM)),
            gather_rows(lambda r: r[1][:, :kv_w], lo, hi, lead + (KV_A, HEAD_DIM)),
            gather_rows(lambda r: r[2][:, :IDX_DIM], lo, hi, lead + (IDX_DIM,)),
            gather_rows(lambda r: r[0][:, kv_w:], lo, hi, lead + (KV_B, HEAD_DIM)),
            gather_rows(lambda r: r[1][:, kv_w:], lo, hi, lead + (KV_B, HEAD_DIM)),
        ]
    return tuple(outs)
```

```python
import functools

import jax
import jax.numpy as jnp
from jax import lax
from jax.experimental import pallas as pl
from jax.experimental.pallas import tpu as pltpu

HEAD_DIM = 128
KV_A = 4
IDX_HEADS = 32
IDX_DIM = 64
DSA_TOPK = 256
KV_B = 4
MOBA_BLOCK = 256
MOBA_TOPK = 3
N_GROUPS = 4
EXPERTS_PER_GROUP = 8
EXPERT_TOPK = 2
ROPE_THETA = 10000.0
LN_EPS = 1e-5

LANE = 128
ROW_ALIGN = 256
MXU_DTYPE = jnp.bfloat16
NEG_BIAS = -1e30
INT_MIN = -2147483648
VMEM_LIMIT = 56 * 1024 * 1024
PAGES_PER_CHUNK = 16
MOE_ROWS = 128
MOE_CHUNKS = 5
MOE_KT = 1024
MOE_NT = 1024


def _cparams(*sem):
    return pltpu.CompilerParams(dimension_semantics=sem, vmem_limit_bytes=VMEM_LIMIT)


def _round_up(n, m):
    return (n + m - 1) // m * m


def _pick_tile(n, cap, mult):
    best = None
    for t in range(mult, cap + 1, mult):
        if n % t == 0:
            best = t
    assert best is not None, (n, cap, mult)
    return best


def _dot_nt(a, b):
    return lax.dot_general(a, b, (((1,), (1,)), ((), ())), preferred_element_type=jnp.float32)


def _ln_kernel(*refs, mode, alpha, router):
    it = iter(refs)
    x_ref = next(it)
    a_ref = next(it) if mode == "add" else None
    g_ref = next(it) if mode == "ple" else None
    p_ref = next(it) if mode == "ple" else None
    gam_ref, bet_ref = next(it), next(it)
    wh_ref = next(it) if router else None
    wl_ref = next(it) if router else None
    o32_ref, o16_ref = next(it), next(it)
    lg_ref = next(it) if router else None

    v = x_ref[...]
    if mode == "add":
        v = alpha * v + a_ref[...]
    elif mode == "ple":
        v = alpha * v + jax.nn.sigmoid(g_ref[...]) * p_ref[...]
    mu = jnp.mean(v, axis=-1, keepdims=True)
    d = v - mu
    var = jnp.mean(d * d, axis=-1, keepdims=True)
    y = d * lax.rsqrt(var + LN_EPS) * gam_ref[...] + bet_ref[...]
    o32_ref[...] = y
    o16_ref[...] = y.astype(o16_ref.dtype)
    if router:
        yh = y.astype(MXU_DTYPE)
        yl = (y - yh.astype(jnp.float32)).astype(MXU_DTYPE)
        wh = wh_ref[...]
        lg = jnp.dot(yh, wh, preferred_element_type=jnp.float32)
        lg = lg + jnp.dot(yl, wh, preferred_element_type=jnp.float32)
        lg = lg + jnp.dot(yh, wl_ref[...], preferred_element_type=jnp.float32)
        lg_ref[...] = lg


def _ln_embed_kernel(xh_ref, xt_ref, gam_ref, bet_ref, o32_ref, o16_ref, *, n_head):
    v = jnp.where(pl.program_id(0) < n_head, xh_ref[...], xt_ref[...])
    mu = jnp.mean(v, axis=-1, keepdims=True)
    d = v - mu
    var = jnp.mean(d * d, axis=-1, keepdims=True)
    y = d * lax.rsqrt(var + LN_EPS) * gam_ref[...] + bet_ref[...]
    o32_ref[...] = y
    o16_ref[...] = y.astype(o16_ref.dtype)


def _layer_norm_embed(x_head, x_tail, gam, bet):
    n_h, D = x_head.shape
    n_t = x_tail.shape[0]
    tr = _pick_tile(ROW_ALIGN, 128, 16)
    assert n_h % tr == 0 and n_t % tr == 0
    nh, ntl = n_h // tr, n_t // tr
    vec = pl.BlockSpec((1, D), lambda i: (0, 0))
    row = pl.BlockSpec((tr, D), lambda i: (i, 0))
    return pl.pallas_call(
        functools.partial(_ln_embed_kernel, n_head=nh), grid=(nh + ntl,),
        in_specs=[pl.BlockSpec((tr, D), lambda i: (jnp.minimum(i, nh - 1), 0)),
                  pl.BlockSpec((tr, D), lambda i: (jnp.maximum(i - nh, 0), 0)), vec, vec],
        out_specs=[row, row],
        out_shape=[jax.ShapeDtypeStruct((n_h + n_t, D), jnp.float32),
                   jax.ShapeDtypeStruct((n_h + n_t, D), MXU_DTYPE)],
        compiler_params=_cparams("parallel"), name="ln_embed")(x_head, x_tail, gam.reshape(1, D), bet.reshape(1, D))


def _layer_norm(xs, gam, bet, mode, alpha=1.0, router_w=None):
    Np, D = xs[0].shape
    tr = _pick_tile(Np, 192, 16)
    row = pl.BlockSpec((tr, D), lambda i: (i, 0))
    vec = pl.BlockSpec((1, D), lambda i: (0, 0))
    in_specs = [row] * len(xs) + [vec, vec]
    args = list(xs) + [gam.reshape(1, D), bet.reshape(1, D)]
    out_shape = [jax.ShapeDtypeStruct((Np, D), jnp.float32), jax.ShapeDtypeStruct((Np, D), MXU_DTYPE)]
    out_specs = [row, row]
    if router_w is not None:
        wspec = pl.BlockSpec((D, LANE), lambda i: (0, 0))
        in_specs += [wspec, wspec]
        args += list(router_w)
        out_shape.append(jax.ShapeDtypeStruct((Np, LANE), jnp.float32))
        out_specs.append(pl.BlockSpec((tr, LANE), lambda i: (i, 0)))
    return pl.pallas_call(
        functools.partial(_ln_kernel, mode=mode, alpha=alpha, router=router_w is not None),
        grid=(Np // tr,), in_specs=in_specs, out_specs=out_specs, out_shape=out_shape,
        compiler_params=_cparams("parallel"), name="ln_" + mode)(*args)


def _mm_kernel(*refs, n_x, mode, slice_major):
    x_refs = refs[:n_x]
    w_refs = refs[n_x:2 * n_x]
    rest = refs[2 * n_x:]
    if mode == "none":
        (o_ref,) = rest
    else:
        cos_ref, sin_ref, o_ref = rest
    y = jnp.dot(x_refs[0][...], w_refs[0][...], preferred_element_type=jnp.float32)
    for k in range(1, n_x):
        y = y + jnp.dot(x_refs[k][...], w_refs[k][...], preferred_element_type=jnp.float32)
    if mode == "none" and not slice_major:
        o_ref[...] = y.astype(o_ref.dtype)
        return
    tm = y.shape[0]
    if mode != "none":
        cos = cos_ref[...]
        sin = sin_ref[...]
    if mode == "rope64":
        lane = lax.broadcasted_iota(jnp.int32, (tm, LANE), 1)
        first_half = (lane % 64) < 32
    for s in range(y.shape[1] // LANE):
        yh = y[:, s * LANE:(s + 1) * LANE]
        if mode == "rope128":
            yh = yh * cos + pltpu.roll(yh, 64, 1) * sin
        elif mode == "rope64":
            partner = jnp.where(first_half, pltpu.roll(yh, 96, 1), pltpu.roll(yh, 32, 1))
            yh = yh * cos + partner * sin
        if slice_major:
            o_ref[s] = yh.astype(o_ref.dtype)
        else:
            o_ref[:, s * LANE:(s + 1) * LANE] = yh.astype(o_ref.dtype)


def _matmul(xs, ws, out_dtype, mode="none", tables=None, slice_major=False, name="mm"):
    Np = xs[0].shape[0]
    Nc = ws[0].shape[1]
    tm = _pick_tile(Np, 640, 16)
    tn = _pick_tile(Nc, 1024, LANE)
    in_specs = [pl.BlockSpec((tm, x.shape[1]), lambda j, i: (i, 0)) for x in xs]
    in_specs += [pl.BlockSpec((w.shape[0], tn), lambda j, i: (0, j)) for w in ws]
    args = list(xs) + list(ws)
    if mode != "none":
        in_specs += [pl.BlockSpec((tm, LANE), lambda j, i: (i, 0))] * 2
        args += list(tables)
    if slice_major:
        out_shape = jax.ShapeDtypeStruct((Nc // LANE, Np, LANE), out_dtype)
        out_spec = pl.BlockSpec((tn // LANE, tm, LANE), lambda j, i: (j, i, 0))
    else:
        out_shape = jax.ShapeDtypeStruct((Np, Nc), out_dtype)
        out_spec = pl.BlockSpec((tm, tn), lambda j, i: (i, j))
    return pl.pallas_call(
        functools.partial(_mm_kernel, n_x=len(xs), mode=mode, slice_major=slice_major),
        grid=(Nc // tn, Np // tm), in_specs=in_specs, out_specs=out_spec, out_shape=out_shape,
        compiler_params=_cparams("parallel", "parallel"), name=name)(*args)


def _float_keys(x):
    b = lax.bitcast_convert_type(x, jnp.int32)
    return jnp.where(b < 0, b ^ jnp.int32(0x7FFFFFFF), b)


def _kth_largest(count_ge, shape, k):
    def body(it, ans):
        cand = ans + lax.shift_left(jnp.int32(1), 31 - it)
        return jnp.where(count_ge(cand) >= k, cand, ans)
    return lax.fori_loop(0, 32, body, jnp.full(shape, INT_MIN, jnp.int32))


def _topk_lanes(gate, k):
    lane = lax.broadcasted_iota(jnp.int32, gate.shape, 1)
    sel = jnp.zeros(gate.shape, jnp.float32)
    g = gate
    for _ in range(k):
        m = jnp.max(g, axis=-1, keepdims=True)
        idx = jnp.min(jnp.where(g == m, lane, LANE), axis=-1, keepdims=True)
        hit = lane == idx
        sel = jnp.where(hit, 1.0, sel)
        g = jnp.where(hit, -jnp.inf, g)
    return sel


def _attend_blocks(q_ref, k_ref, v_ref, bias_ref, o_ref, rep, scale, n_blocks, bk):
    tq = o_ref.shape[0]
    for r in range(rep):
        q = q_ref[r]

        def body(n, carry):
            m, l, acc = carry
            rows = pl.ds(pl.multiple_of(n * bk, bk), bk)
            s = _dot_nt(q, k_ref[rows, :].astype(MXU_DTYPE)) * scale + bias_ref[n]
            m_new = jnp.maximum(m, jnp.max(s, axis=-1, keepdims=True))
            a = jnp.exp(m - m_new)
            p = jnp.exp(s - m_new)
            pv = jnp.dot(p.astype(MXU_DTYPE), v_ref[rows, :].astype(MXU_DTYPE), preferred_element_type=jnp.float32)
            return m_new, a * l + jnp.sum(p, axis=-1, keepdims=True), a * acc + pv

        init = (jnp.full((tq, 1), -jnp.inf, jnp.float32), jnp.zeros((tq, 1), jnp.float32),
                jnp.zeros((tq, HEAD_DIM), jnp.float32))
        _, l, acc = lax.fori_loop(0, n_blocks, body, init)
        o_ref[:, r * HEAD_DIM:(r + 1) * HEAD_DIM] = (acc / l).astype(o_ref.dtype)


def _lane_fold(x):
    out = x[:, :LANE]
    for t in range(1, x.shape[1] // LANE):
        out = out + x[:, t * LANE:(t + 1) * LANE]
    return out


def _dsa_prompt_kernel(q_ref, k_ref, v_ref, qi_ref, wq_ref, kiw_ref, o_ref, bias_ref, key_ref, kab_ref,
                       *, tq, T, rep, n_keep, scale):
    i = pl.program_id(1)
    g = pl.program_id(2)
    nb = i + 1

    @pl.when(g == 0)
    def _():
        klane = lax.broadcasted_iota(jnp.int32, (T, LANE), 1)
        ka32 = jnp.where(klane < IDX_DIM, kiw_ref[...], 0.0)
        kab_ref[0] = ka32.astype(MXU_DTYPE)
        kab_ref[1] = pltpu.roll(ka32, IDX_DIM, 1).astype(MXU_DTYPE)
        wq = wq_ref[...]
        row = lax.broadcasted_iota(jnp.int32, (tq, tq), 0) + i * tq
        col = lax.broadcasted_iota(jnp.int32, (tq, tq), 1)

        def score_block(n, carry):
            rows = pl.ds(pl.multiple_of(n * tq, tq), tq)
            ka = kab_ref[0, rows, :]
            kb = kab_ref[1, rows, :]
            score = None
            for hp in range(IDX_HEADS // 2):
                qp = qi_ref[hp]
                c = (jnp.maximum(_dot_nt(qp, ka), 0.0) * wq[:, IDX_DIM + 2 * hp:IDX_DIM + 2 * hp + 1]
                     + jnp.maximum(_dot_nt(qp, kb), 0.0) * wq[:, IDX_DIM + 2 * hp + 1:IDX_DIM + 2 * hp + 2])
                score = c if score is None else score + c
            key_ref[n] = jnp.where(n * tq + col <= row, _float_keys(score), INT_MIN)
            return carry
        lax.fori_loop(0, nb, score_block, 0)

        def count(pred):
            def body(n, acc):
                return acc + _lane_fold(jnp.where(pred(key_ref[n], n), 1, 0))
            acc = lax.fori_loop(0, nb, body, jnp.zeros((tq, LANE), jnp.int32))
            return jnp.sum(acc, axis=-1, keepdims=True)

        thr = _kth_largest(lambda cand: count(lambda k, n: k >= cand), (tq, 1), n_keep)
        n_gt = count(lambda k, n: k > thr)
        n_eq = count(lambda k, n: (k == thr) & (k > INT_MIN))

        def write_bias(eq_ok):
            def body(n, carry):
                k = key_ref[n]
                keep = (k > thr) | ((k == thr) & (k > INT_MIN) & eq_ok(n))
                bias_ref[n] = jnp.where(keep, 0.0, NEG_BIAS)
                return carry
            lax.fori_loop(0, nb, body, 0)

        tie = (thr > INT_MIN) & (n_gt + n_eq > n_keep)
        any_tie = jnp.max(jnp.where(tie, 1, 0)) > 0

        @pl.when(jnp.logical_not(any_tie))
        def _():
            write_bias(lambda n: True)

        @pl.when(any_tie)
        def _():
            need = n_keep - n_gt
            nbits = T.bit_length()

            def body(it, lim):
                cand = lim + lax.shift_left(jnp.int32(1), nbits - 1 - it)
                c = count(lambda k, n: (k == thr) & (k > INT_MIN) & (n * tq + col < cand))
                return jnp.where(c <= need, cand, lim)

            lim = lax.fori_loop(0, nbits, body, jnp.zeros((tq, 1), jnp.int32))
            write_bias(lambda n: n * tq + col < lim)

    _attend_blocks(q_ref, k_ref, v_ref, bias_ref, o_ref, rep, scale, nb, tq)


def _dsa_prompt(q128, k128, v128, qi, kiwi, B, T, rep):
    tq = min(256, T)
    assert T % tq == 0 and tq % LANE == 0
    nT = T // tq
    n_keep = min(DSA_TOPK, T // 4)
    kern = functools.partial(_dsa_prompt_kernel, tq=tq, T=T, rep=rep, n_keep=n_keep,
                             scale=HEAD_DIM ** -0.5)
    return pl.pallas_call(
        kern, grid=(B, nT, KV_A),
        in_specs=[
            pl.BlockSpec((rep, tq, HEAD_DIM), lambda b, i, g: (g, b * nT + i, 0)),
            pl.BlockSpec((T, HEAD_DIM), lambda b, i, g: (b, g)),
            pl.BlockSpec((T, HEAD_DIM), lambda b, i, g: (b, g)),
            pl.BlockSpec((IDX_HEADS // 2, tq, LANE), lambda b, i, g: (0, b * nT + i, 0)),
            pl.BlockSpec((tq, LANE), lambda b, i, g: (b * nT + i, 0)),
            pl.BlockSpec((T, LANE), lambda b, i, g: (b, 0)),
        ],
        out_specs=pl.BlockSpec((tq, rep * HEAD_DIM), lambda b, i, g: (b * nT + i, g)),
        out_shape=jax.ShapeDtypeStruct((B * T, KV_A * rep * HEAD_DIM), MXU_DTYPE),
        scratch_shapes=[pltpu.VMEM((nT, tq, tq), jnp.float32), pltpu.VMEM((nT, tq, tq), jnp.int32),
                        pltpu.VMEM((2, T, LANE), MXU_DTYPE)],
        compiler_params=_cparams("parallel", "parallel", "arbitrary"), name="dsa_prompt",
    )(q128, k128, v128, qi, kiwi, kiwi)


def _moba_prompt_kernel(q_ref, k_ref, v_ref, o_ref, bias_ref, *, bs, nblk, rep, n_pick, scale):
    i = pl.program_id(1)
    kf = k_ref[...]
    qsum = q_ref[0].astype(jnp.float32)
    for r in range(1, rep):
        qsum = qsum + q_ref[r].astype(jnp.float32)
    lane = lax.broadcasted_iota(jnp.int32, (bs, LANE), 1)
    gate = jnp.full((bs, LANE), -jnp.inf, jnp.float32)
    for n in range(nblk):
        mean_n = jnp.mean(kf[n * bs:(n + 1) * bs], axis=0, keepdims=True)
        gate = jnp.where(lane == n, jnp.sum(qsum * mean_n, axis=-1, keepdims=True), gate)
    past = lane < i
    sel = jnp.where(past, _topk_lanes(jnp.where(past, gate, -jnp.inf), n_pick), 0.0)
    row = lax.broadcasted_iota(jnp.int32, (bs, bs), 0)
    col = lax.broadcasted_iota(jnp.int32, (bs, bs), 1)
    own_bias = jnp.where(col <= row, 0.0, NEG_BIAS)
    for n in range(nblk):
        picked = jnp.where(sel[:, n:n + 1] > 0.5, 0.0, NEG_BIAS)
        bias_ref[n] = jnp.where(i == n, own_bias, jnp.broadcast_to(picked, (bs, bs)))
    _attend_blocks(q_ref, k_ref, v_ref, bias_ref, o_ref, rep, scale, i + 1, bs)


def _moba_prompt(q128, k128, v128, B, T, rep, h_a):
    bs = MOBA_BLOCK
    assert T % bs == 0
    nblk = T // bs
    assert 1 <= nblk <= LANE
    kern = functools.partial(_moba_prompt_kernel, bs=bs, nblk=nblk, rep=rep,
                             n_pick=min(MOBA_TOPK, nblk), scale=HEAD_DIM ** -0.5)
    return pl.pallas_call(
        kern, grid=(B, nblk, KV_B),
        in_specs=[
            pl.BlockSpec((rep, bs, HEAD_DIM), lambda b, i, g: (h_a // rep + g, b * nblk + i, 0)),
            pl.BlockSpec((T, HEAD_DIM), lambda b, i, g: (b, KV_A + g)),
            pl.BlockSpec((T, HEAD_DIM), lambda b, i, g: (b, KV_A + g)),
        ],
        out_specs=pl.BlockSpec((bs, rep * HEAD_DIM), lambda b, i, g: (b * nblk + i, g)),
        out_shape=jax.ShapeDtypeStruct((B * T, KV_B * rep * HEAD_DIM), MXU_DTYPE),
        scratch_shapes=[pltpu.VMEM((nblk, bs, bs), jnp.float32)],
        compiler_params=_cparams("parallel", "parallel", "parallel"), name="moba_prompt",
    )(q128, k128, v128)


def _page_copies(pt_ref, pool_ref, layer, buf, sem, b, chunk, slot, pages, page, groups=0):
    out = []
    for p in range(pages):
        phys = pt_ref[b, chunk * pages + p]
        rows = pl.ds(p * page, page)
        if not groups:
            out.append(pltpu.make_async_copy(pool_ref.at[layer, phys], buf.at[slot, rows], sem.at[slot]))
        for g in range(groups):
            out.append(pltpu.make_async_copy(pool_ref.at[layer, phys, :, g, :], buf.at[slot, g, rows], sem.at[slot]))
    return out


def _paged_step(c, nchunk, start_fn, wait_fn):
    @pl.when(c == 0)
    def _():
        start_fn(0, 0)

    slot = c % 2
    wait_fn(c, slot)

    @pl.when(c + 1 < nchunk)
    def _():
        start_fn(c + 1, 1 - slot)

    return slot


def _dsa_decode_mask_kernel(pt_ref, qi_ref, wc_ref, pool_ref, knew_ref, bp_ref, bn_ref,
                            kbuf, sem, sp_ref, sn_ref, *, layer, nchunk, pages, page, Td, n_keep, past):
    b = pl.program_id(0)
    c = pl.program_id(1)
    CH = pages * page

    def start_fn(cc, slot):
        for cp in _page_copies(pt_ref, pool_ref, layer, kbuf, sem, b, cc, slot, pages, page):
            cp.start()

    def wait_fn(cc, slot):
        for cp in _page_copies(pt_ref, pool_ref, layer, kbuf, sem, b, cc, slot, pages, page):
            cp.wait()

    def scores(kc, width):
        d = _dot_nt(qi_ref[0], kc.astype(MXU_DTYPE))
        r = jnp.maximum(d, 0.0) * wc_ref[0]
        return jnp.sum(r.reshape(IDX_HEADS, Td, width), axis=0)

    @pl.when(c < nchunk)
    def _():
        slot = _paged_step(c, nchunk, start_fn, wait_fn)
        sp_ref[c] = scores(kbuf[slot], CH)

    @pl.when(c == nchunk)
    def _():
        sn_ref[...] = scores(knew_ref[0], LANE)
        trow = lax.broadcasted_iota(jnp.int32, (Td, LANE), 0)
        ncol = lax.broadcasted_iota(jnp.int32, (Td, LANE), 1)
        new_ok = ncol <= trow
        kp = _float_keys(sp_ref[...])
        kn = jnp.where(new_ok, _float_keys(sn_ref[...]), INT_MIN)

        def count(mp, mn):
            return (jnp.sum(jnp.sum(jnp.where(mp, 1, 0), axis=0), axis=-1, keepdims=True)
                    + jnp.sum(jnp.where(mn, 1, 0), axis=-1, keepdims=True))

        thr = _kth_largest(lambda cand: count(kp >= cand[None], kn >= cand), (Td, 1), n_keep)
        gt_p, eq_p = kp > thr[None], kp == thr[None]
        gt_n, eq_n = kn > thr, kn == thr
        n_gt = count(gt_p, gt_n)
        n_eq = count(eq_p, eq_n)
        bp_ref[0] = jnp.where(gt_p | eq_p, 0.0, NEG_BIAS)
        bn_ref[0] = jnp.where((gt_n | eq_n) & new_ok, 0.0, NEG_BIAS)
        tie = (thr > INT_MIN) & (n_gt + n_eq > n_keep)

        @pl.when(jnp.max(jnp.where(tie, 1, 0)) > 0)
        def _():
            need = n_keep - n_gt
            pos_p = (lax.broadcasted_iota(jnp.int32, (nchunk, Td, CH), 0) * CH
                     + lax.broadcasted_iota(jnp.int32, (nchunk, Td, CH), 2))
            pos_n = past + ncol
            nbits = (past + LANE).bit_length()

            def body(it, lim):
                cand = lim + lax.shift_left(jnp.int32(1), nbits - 1 - it)
                cnt = count(eq_p & (pos_p < cand[None]), eq_n & (pos_n < cand))
                return jnp.where(cnt <= need, cand, lim)

            lim = lax.fori_loop(0, nbits, body, jnp.zeros((Td, 1), jnp.int32))
            bp_ref[0] = jnp.where(gt_p | (eq_p & (pos_p < lim[None])), 0.0, NEG_BIAS)
            bn_ref[0] = jnp.where((gt_n | (eq_n & (pos_n < lim))) & new_ok, 0.0, NEG_BIAS)


def _dsa_decode_mask(page_table, qi_s, wcol, pool, knew, layer, Td):
    Bd, n_pages = page_table.shape
    page = pool.shape[2]
    pages = min(PAGES_PER_CHUNK, n_pages)
    assert n_pages % pages == 0
    nchunk = n_pages // pages
    CH = pages * page
    past = n_pages * page
    n_keep = min(DSA_TOPK, (past + Td) // 4)
    HT = qi_s.shape[1]
    kern = functools.partial(_dsa_decode_mask_kernel, layer=layer, nchunk=nchunk, pages=pages, page=page,
                             Td=Td, n_keep=n_keep, past=past)
    grid_spec = pltpu.PrefetchScalarGridSpec(
        num_scalar_prefetch=1, grid=(Bd, nchunk + 1),
        in_specs=[
            pl.BlockSpec((1, HT, IDX_DIM), lambda b, c, pt: (b, 0, 0)),
            pl.BlockSpec((1, HT, 1), lambda b, c, pt: (b, 0, 0)),
            pl.BlockSpec(memory_space=pl.ANY),
            pl.BlockSpec((1, LANE, IDX_DIM), lambda b, c, pt: (b, 0, 0)),
        ],
        out_specs=[
            pl.BlockSpec((1, nchunk, Td, CH), lambda b, c, pt: (b, 0, 0, 0)),
            pl.BlockSpec((1, Td, LANE), lambda b, c, pt: (b, 0, 0)),
        ],
        scratch_shapes=[
            pltpu.VMEM((2, CH, IDX_DIM), jnp.float32),
            pltpu.SemaphoreType.DMA((2,)),
            pltpu.VMEM((nchunk, Td, CH), jnp.float32),
            pltpu.VMEM((Td, LANE), jnp.float32),
        ])
    return pl.pallas_call(
        kern, grid_spec=grid_spec,
        out_shape=[jax.ShapeDtypeStruct((Bd, nchunk, Td, CH), jnp.float32),
                   jax.ShapeDtypeStruct((Bd, Td, LANE), jnp.float32)],
        compiler_params=_cparams("arbitrary", "arbitrary"), name="dsa_decode_mask",
    )(page_table, qi_s, wcol, pool, knew)


def _moba_decode_mask_kernel(pt_ref, q_ref, pool_ref, bp_ref, kbuf, sem, gate_ref,
                             *, layer, nchunk, pages, page, Td, rep, bs, n_pick):
    b = pl.program_id(0)
    c = pl.program_id(1)
    CH = pages * page
    per = CH // bs
    nblk = nchunk * per

    def start_fn(cc, slot):
        for cp in _page_copies(pt_ref, pool_ref, layer, kbuf, sem, b, cc, slot, pages, page, KV_B):
            cp.start()

    def wait_fn(cc, slot):
        for cp in _page_copies(pt_ref, pool_ref, layer, kbuf, sem, b, cc, slot, pages, page, KV_B):
            cp.wait()

    @pl.when(c == 0)
    def _():
        gate_ref[...] = jnp.full(gate_ref.shape, -jnp.inf, jnp.float32)

    slot = _paged_step(c, nchunk, start_fn, wait_fn)
    lane = lax.broadcasted_iota(jnp.int32, (Td, LANE), 1)
    for g in range(KV_B):
        qsum = jnp.sum(q_ref[0, g].astype(jnp.float32).reshape(rep, Td, HEAD_DIM), axis=0)
        gate = gate_ref[g]
        for nb in range(per):
            mean_nb = jnp.mean(kbuf[slot, g, nb * bs:(nb + 1) * bs, :], axis=0, keepdims=True)
            gate = jnp.where(lane == c * per + nb, jnp.sum(qsum * mean_nb, axis=-1, keepdims=True), gate)
        gate_ref[g] = gate

    @pl.when(c == nchunk - 1)
    def _():
        for g in range(KV_B):
            sel = jnp.where(lane < nblk, _topk_lanes(gate_ref[g], n_pick), 0.0)
            for n in range(nblk):
                picked = jnp.where(sel[:, n:n + 1] > 0.5, 0.0, NEG_BIAS)
                bp_ref[0, n // per, g, :, (n % per) * bs:(n % per + 1) * bs] = jnp.broadcast_to(picked, (Td, bs))


def _moba_decode_mask(page_table, q_s, pool, layer, Td, rep):
    Bd, n_pages = page_table.shape
    page = pool.shape[2]
    pages = min(PAGES_PER_CHUNK, n_pages)
    assert n_pages % pages == 0
    nchunk = n_pages // pages
    CH = pages * page
    bs = MOBA_BLOCK
    assert CH % bs == 0 and Td <= bs
    nblk = n_pages * page // bs
    assert nblk <= LANE
    kern = functools.partial(_moba_decode_mask_kernel, layer=layer, nchunk=nchunk, pages=pages, page=page,
                             Td=Td, rep=rep, bs=bs, n_pick=min(MOBA_TOPK, (n_pages * page + Td) // bs))
    grid_spec = pltpu.PrefetchScalarGridSpec(
        num_scalar_prefetch=1, grid=(Bd, nchunk),
        in_specs=[
            pl.BlockSpec((1, KV_B, rep * Td, HEAD_DIM), lambda b, c, pt: (b, 0, 0, 0)),
            pl.BlockSpec(memory_space=pl.ANY),
        ],
        out_specs=pl.BlockSpec((1, nchunk, KV_B, Td, CH), lambda b, c, pt: (b, 0, 0, 0, 0)),
        scratch_shapes=[
            pltpu.VMEM((2, KV_B, CH, HEAD_DIM), jnp.float32),
            pltpu.SemaphoreType.DMA((2,)),
            pltpu.VMEM((KV_B, Td, LANE), jnp.float32),
        ])
    return pl.pallas_call(
        kern, grid_spec=grid_spec,
        out_shape=jax.ShapeDtypeStruct((Bd, nchunk, KV_B, Td, CH), jnp.float32),
        compiler_params=_cparams("arbitrary", "arbitrary"), name="moba_decode_mask",
    )(page_table, q_s, pool)


def _paged_attn_kernel(pt_ref, q_ref, kpool_ref, vpool_ref, knew_ref, vnew_ref, bp_ref, bn_ref, o_ref,
                       kbuf, vbuf, sem, m_ref, l_ref, acc_ref,
                       *, layer, nchunk, pages, page, Td, rep, n_kv, per_group_bias, scale):
    b = pl.program_id(0)
    c = pl.program_id(1)

    def start_fn(cc, slot):
        for cp in (_page_copies(pt_ref, kpool_ref, layer, kbuf, sem.at[0], b, cc, slot, pages, page, n_kv)
                   + _page_copies(pt_ref, vpool_ref, layer, vbuf, sem.at[1], b, cc, slot, pages, page, n_kv)):
            cp.start()

    def wait_fn(cc, slot):
        for cp in (_page_copies(pt_ref, kpool_ref, layer, kbuf, sem.at[0], b, cc, slot, pages, page, n_kv)
                   + _page_copies(pt_ref, vpool_ref, layer, vbuf, sem.at[1], b, cc, slot, pages, page, n_kv)):
            cp.wait()

    @pl.when(c == 0)
    def _():
        m_ref[...] = jnp.full(m_ref.shape, -jnp.inf, jnp.float32)
        l_ref[...] = jnp.zeros(l_ref.shape, jnp.float32)
        acc_ref[...] = jnp.zeros(acc_ref.shape, jnp.float32)

    def process(k_fn, v_fn, bias_fn):
        for g in range(n_kv):
            k = k_fn(g).astype(MXU_DTYPE)
            v = v_fn(g).astype(MXU_DTYPE)
            bias = bias_fn(g)
            s = _dot_nt(q_ref[0, g], k) * scale + jnp.concatenate([bias] * rep, axis=0)
            m_old = m_ref[g]
            m_new = jnp.maximum(m_old, jnp.max(s, axis=-1, keepdims=True))
            a = jnp.exp(m_old - m_new)
            p = jnp.exp(s - m_new)
            l_ref[g] = a * l_ref[g] + jnp.sum(p, axis=-1, keepdims=True)
            acc_ref[g] = a * acc_ref[g] + jnp.dot(p.astype(MXU_DTYPE), v, preferred_element_type=jnp.float32)
            m_ref[g] = m_new

    @pl.when(c < nchunk)
    def _():
        slot = _paged_step(c, nchunk, start_fn, wait_fn)
        process(lambda g: kbuf[slot, g], lambda g: vbuf[slot, g],
                lambda g: bp_ref[0, 0, g if per_group_bias else 0])

    @pl.when(c == nchunk)
    def _():
        process(lambda g: knew_ref[0, :, g * HEAD_DIM:(g + 1) * HEAD_DIM],
                lambda g: vnew_ref[0, :, g * HEAD_DIM:(g + 1) * HEAD_DIM], lambda g: bn_ref[0])
        for g in range(n_kv):
            o_ref[0, g] = acc_ref[g] / l_ref[g]


def _paged_attention(page_table, q_s, kpool, vpool, knew, vnew, bias_past, bias_new, layer, Td, rep):
    Bd, n_pages = page_table.shape
    page = kpool.shape[2]
    pages = min(PAGES_PER_CHUNK, n_pages)
    nchunk = n_pages // pages
    CH = pages * page
    n_kv = q_s.shape[1]
    R = rep * Td
    W = n_kv * HEAD_DIM
    gb = bias_past.shape[2]
    nb_new = bias_new.shape[0]
    kern = functools.partial(_paged_attn_kernel, layer=layer, nchunk=nchunk, pages=pages, page=page, Td=Td,
                             rep=rep, n_kv=n_kv, per_group_bias=gb > 1, scale=HEAD_DIM ** -0.5)
    grid_spec = pltpu.PrefetchScalarGridSpec(
        num_scalar_prefetch=1, grid=(Bd, nchunk + 1),
        in_specs=[
            pl.BlockSpec((1, n_kv, R, HEAD_DIM), lambda b, c, pt: (b, 0, 0, 0)),
            pl.BlockSpec(memory_space=pl.ANY),
            pl.BlockSpec(memory_space=pl.ANY),
            pl.BlockSpec((1, LANE, W), lambda b, c, pt: (b, 0, 0)),
            pl.BlockSpec((1, LANE, W), lambda b, c, pt: (b, 0, 0)),
            pl.BlockSpec((1, 1, gb, Td, CH), lambda b, c, pt: (b, jnp.minimum(c, nchunk - 1), 0, 0, 0)),
            pl.BlockSpec((1, Td, LANE), lambda b, c, pt: (b if nb_new > 1 else 0, 0, 0)),
        ],
        out_specs=pl.BlockSpec((1, n_kv, R, HEAD_DIM), lambda b, c, pt: (b, 0, 0, 0)),
        scratch_shapes=[
            pltpu.VMEM((2, n_kv, CH, HEAD_DIM), jnp.float32),
            pltpu.VMEM((2, n_kv, CH, HEAD_DIM), jnp.float32),
            pltpu.SemaphoreType.DMA((2, 2)),
            pltpu.VMEM((n_kv, R, 1), jnp.float32),
            pltpu.VMEM((n_kv, R, 1), jnp.float32),
            pltpu.VMEM((n_kv, R, HEAD_DIM), jnp.float32),
        ])
    return pl.pallas_call(
        kern, grid_spec=grid_spec,
        out_shape=jax.ShapeDtypeStruct((Bd, n_kv, R, HEAD_DIM), jnp.float32),
        compiler_params=_cparams("arbitrary", "arbitrary"), name="paged_attention",
    )(page_table, q_s, kpool, vpool, knew, vnew, bias_past, bias_new)


def _route_kernel(lg_ref, eid_ref, gate_ref):
    lg = lg_ref[...]
    lane = lax.broadcasted_iota(jnp.int32, lg.shape, 1)
    gmask = lane < N_GROUPS
    gl = jnp.where(gmask, lg, -jnp.inf)
    gmax = jnp.max(gl, axis=-1, keepdims=True)
    gsel = jnp.min(jnp.where(gl == gmax, lane, LANE), axis=-1, keepdims=True)
    g_w = 1.0 / jnp.sum(jnp.where(gmask, jnp.exp(gl - gmax), 0.0), axis=-1, keepdims=True)
    lo = N_GROUPS + gsel * EXPERTS_PER_GROUP
    el = jnp.where((lane >= lo) & (lane < lo + EXPERTS_PER_GROUP), lg, -jnp.inf)
    v1 = jnp.max(el, axis=-1, keepdims=True)
    i1 = jnp.min(jnp.where(el == v1, lane, LANE), axis=-1, keepdims=True)
    el2 = jnp.where(lane == i1, -jnp.inf, el)
    v2 = jnp.max(el2, axis=-1, keepdims=True)
    i2 = jnp.min(jnp.where(el2 == v2, lane, LANE), axis=-1, keepdims=True)
    e2 = jnp.exp(v2 - v1)
    p1 = 1.0 / (1.0 + e2)
    eid_ref[...] = jnp.where(lane == 0, i1 - N_GROUPS, jnp.where(lane == 1, i2 - N_GROUPS, 0))
    gate_ref[...] = jnp.where(lane == 0, g_w * p1, jnp.where(lane == 1, g_w * (e2 * p1), 0.0))


def _route(logits):
    Np = logits.shape[0]
    tr = _pick_tile(Np, 1024, 8)
    spec = pl.BlockSpec((tr, LANE), lambda i: (i, 0))
    return pl.pallas_call(
        _route_kernel, grid=(Np // tr,), in_specs=[spec], out_specs=[spec, spec],
        out_shape=[jax.ShapeDtypeStruct((Np, LANE), jnp.int32), jax.ShapeDtypeStruct((Np, LANE), jnp.float32)],
        compiler_params=_cparams("parallel"), name="route")(logits)


def _moe_plan(eid, n_experts, rc, nch_max):
    M = eid.shape[0]
    C = rc * nch_max
    order = jnp.argsort(eid, stable=True).astype(jnp.int32)
    eid_s = eid[order]
    counts = jnp.bincount(eid, length=n_experts).astype(jnp.int32)
    start = jnp.cumsum(counts) - counts
    pcounts = (counts + rc - 1) // rc * rc
    pstart = jnp.cumsum(pcounts) - pcounts
    slot_s = (pstart[eid_s] + jnp.arange(M, dtype=jnp.int32) - start[eid_s]).astype(jnp.int32)
    R = _round_up(M, rc) + n_experts * rc
    pend = pstart + pcounts
    slots = jnp.arange(R, dtype=jnp.int32)
    e_slot = jnp.minimum(jnp.searchsorted(pend, slots, side="right"), n_experts - 1)
    j_slot = slots - pstart[e_slot]
    src = jnp.clip(start[e_slot] + j_slot, 0, M - 1)
    tok = jnp.where((j_slot < counts[e_slot]) & (slots < pend[-1]), order[src] // EXPERT_TOPK, 0)
    dest = slot_s[jnp.argsort(order)]
    nseg = (pcounts + C - 1) // C
    send = jnp.cumsum(nseg)
    sstart = send - nseg
    n_seg = -(-R // C) + n_experts
    sidx = jnp.arange(n_seg, dtype=jnp.int32)
    e_of = jnp.minimum(jnp.searchsorted(send, sidx, side="right"), n_experts - 1).astype(jnp.int32)
    active = sidx < send[-1]
    local = sidx - sstart[e_of]
    nch = jnp.where(active, jnp.clip((pcounts[e_of] - local * C + rc - 1) // rc, 0, nch_max), 0)
    used = pstart[-1] + pcounts[-1]
    idle_row0 = used + (sidx - send[-1]) * C
    row0 = jnp.where(active, pstart[e_of] + local * C, jnp.minimum(idle_row0, R))
    nzero = jnp.where(active, 0, jnp.clip((R - idle_row0) // rc, 0, nch_max))
    seg_e = jnp.where(active, e_of, e_of[jnp.maximum(send[-1] - 1, 0)])
    i32 = jnp.int32
    return tok, dest, seg_e.astype(i32), row0.astype(i32), nch.astype(i32), nzero.astype(i32), R


def _moe_kernel(seg_e_ref, row0_ref, nch_ref, nzero_ref, tok_ref, x_hbm, wg_ref, wu_ref, wd_ref, y_hbm,
                stage, xbuf, a_acc, u_acc, hbuf, wgb, wub, wdb, ostage, gsem, osem, *, rc, nk, nn):
    del seg_e_ref
    s = pl.program_id(0)
    j = pl.program_id(1)
    nch = nch_ref[s]
    row0 = row0_ref[s]
    nzero = nzero_ref[s]
    kt = xbuf.shape[2]
    nt = ostage.shape[2]

    def chunk_rows(ch):
        return pl.ds(pl.multiple_of(ch * rc, rc), rc)

    def out_copy(slot, ch, n):
        dst = y_hbm.at[pl.ds(pl.multiple_of(row0 + ch * rc, rc), rc), pl.ds(n * nt, nt)]
        return pltpu.make_async_copy(ostage.at[slot], dst, osem.at[slot])

    @pl.when((j == 0) & (nzero > 0))
    def _():
        ostage[0] = jnp.zeros(ostage.shape[1:], jnp.float32)

        def start(ch, carry):
            for n in range(nn):
                out_copy(0, ch, n).start()
            return carry

        def wait(ch, carry):
            for n in range(nn):
                out_copy(0, ch, n).wait()
            return carry
        lax.fori_loop(0, nzero, start, 0)
        lax.fori_loop(0, nzero, wait, 0)

    def gather(ch, slot, wait):
        def body(r, carry):
            tok = tok_ref[row0 + ch * rc + r]
            cp = pltpu.make_async_copy(x_hbm.at[pl.ds(tok, 1)], stage.at[slot, pl.ds(r, 1)], gsem.at[slot])
            if wait:
                cp.wait()
            else:
                cp.start()
            return carry
        lax.fori_loop(0, rc, body, 0, unroll=8)

    @pl.when((j == 0) & (nch > 0))
    def _():
        gather(0, 0, False)

        def body(ch, carry):
            slot = ch % 2
            gather(ch, slot, True)

            @pl.when(ch + 1 < nch)
            def _():
                gather(ch + 1, 1 - slot, False)

            rows = chunk_rows(ch)
            for kk in range(nk):
                xbuf[kk, rows, :] = stage[slot, :, kk * kt:(kk + 1) * kt].astype(xbuf.dtype)
            a_acc[rows, :] = jnp.zeros((rc, a_acc.shape[1]), jnp.float32)
            u_acc[rows, :] = jnp.zeros((rc, u_acc.shape[1]), jnp.float32)
            return carry
        lax.fori_loop(0, nch, body, 0)

    @pl.when((j < nk) & (nch > 0))
    def _():
        wgb[...] = wg_ref[...].astype(wgb.dtype)
        wub[...] = wu_ref[...].astype(wub.dtype)

        def body(ch, carry):
            rows = chunk_rows(ch)
            x = xbuf[j, rows, :]
            a_acc[rows, :] += jnp.dot(x, wgb[...], preferred_element_type=jnp.float32)
            u_acc[rows, :] += jnp.dot(x, wub[...], preferred_element_type=jnp.float32)
            return carry
        lax.fori_loop(0, nch, body, 0)

    @pl.when((j == nk - 1) & (nch > 0))
    def _():
        def body(ch, carry):
            rows = chunk_rows(ch)
            a = a_acc[rows, :]
            hbuf[rows, :] = (a * jax.nn.sigmoid(a) * u_acc[rows, :]).astype(hbuf.dtype)
            return carry
        lax.fori_loop(0, nch, body, 0)

    @pl.when((j >= nk) & (nch > 0))
    def _():
        wdb[...] = wd_ref[...].astype(wdb.dtype)

    for n in range(nn):
        @pl.when((j == nk + n) & (nch > 0))
        def _(n=n):
            def body(ch, carry):
                slot = ch % 2

                @pl.when(ch >= 2)
                def _():
                    out_copy(slot, ch - 2, n).wait()

                ostage[slot] = jnp.dot(hbuf[chunk_rows(ch), :], wdb[...], preferred_element_type=jnp.float32)
                out_copy(slot, ch, n).start()
                return carry
            lax.fori_loop(0, nch, body, 0)

            @pl.when(nch >= 2)
            def _():
                out_copy(nch % 2, nch - 2, n).wait()

            out_copy((nch - 1) % 2, nch - 1, n).wait()


def _moe_experts(x32, plan, w_gate, w_up, w_down, layer):
    tok, _, seg_e, row0, nch, nzero, R = plan
    Np, D = x32.shape
    F = w_gate.shape[-1]
    kt = min(MOE_KT, D)
    nt = min(MOE_NT, D)
    nk, nn = D // kt, D // nt
    rc = MOE_ROWS
    C = rc * MOE_CHUNKS
    n_seg = seg_e.shape[0]

    def k_idx(s, j, nc):
        return jnp.where(nc[s] > 0, jnp.minimum(j, nk - 1), nk - 1)

    def n_idx(s, j, nc):
        return jnp.where(nc[s] > 0, jnp.maximum(j - nk, 0), nn - 1)

    grid_spec = pltpu.PrefetchScalarGridSpec(
        num_scalar_prefetch=5, grid=(n_seg, nk + nn),
        in_specs=[
            pl.BlockSpec(memory_space=pl.ANY),
            pl.BlockSpec((None, None, kt, F), lambda s, j, se, r0, nc, nz, tk: (layer, se[s], k_idx(s, j, nc), 0)),
            pl.BlockSpec((None, None, kt, F), lambda s, j, se, r0, nc, nz, tk: (layer, se[s], k_idx(s, j, nc), 0)),
            pl.BlockSpec((None, None, F, nt), lambda s, j, se, r0, nc, nz, tk: (layer, se[s], 0, n_idx(s, j, nc))),
        ],
        out_specs=pl.BlockSpec(memory_space=pl.ANY),
        scratch_shapes=[
            pltpu.VMEM((2, rc, D), jnp.float32),
            pltpu.VMEM((nk, C, kt), MXU_DTYPE),
            pltpu.VMEM((C, F), jnp.float32),
            pltpu.VMEM((C, F), jnp.float32),
            pltpu.VMEM((C, F), MXU_DTYPE),
            pltpu.VMEM((kt, F), MXU_DTYPE),
            pltpu.VMEM((kt, F), MXU_DTYPE),
            pltpu.VMEM((F, nt), MXU_DTYPE),
            pltpu.VMEM((2, rc, nt), jnp.float32),
            pltpu.SemaphoreType.DMA((2,)),
            pltpu.SemaphoreType.DMA((2,)),
        ])
    return pl.pallas_call(
        functools.partial(_moe_kernel, rc=rc, nk=nk, nn=nn), grid_spec=grid_spec,
        out_shape=jax.ShapeDtypeStruct((R, D), jnp.float32),
        compiler_params=_cparams("arbitrary", "arbitrary"), name="moe_experts",
    )(seg_e, row0, nch, nzero, tok, x32, w_gate, w_up, w_down)


def _combine_kernel(dest_ref, x_ref, gate_ref, gam_ref, bet_ref, y_hbm, o32_ref, o16_ref, ybuf, sem,
                    *, tc, alpha):
    i = pl.program_id(0)

    def rows(wait):
        def body(r, carry):
            for k in range(EXPERT_TOPK):
                slot = dest_ref[(i * tc + r) * EXPERT_TOPK + k]
                cp = pltpu.make_async_copy(y_hbm.at[pl.ds(slot, 1)], ybuf.at[k, pl.ds(r, 1)], sem)
                if wait:
                    cp.wait()
                else:
                    cp.start()
            return carry
        lax.fori_loop(0, tc, body, 0)

    rows(False)
    rows(True)
    gate = gate_ref[...]
    ffn = gate[:, 0:1] * ybuf[0]
    for k in range(1, EXPERT_TOPK):
        ffn = ffn + gate[:, k:k + 1] * ybuf[k]
    v = alpha * x_ref[...] + ffn
    mu = jnp.mean(v, axis=-1, keepdims=True)
    d = v - mu
    var = jnp.mean(d * d, axis=-1, keepdims=True)
    y = d * lax.rsqrt(var + LN_EPS) * gam_ref[...] + bet_ref[...]
    o32_ref[...] = y
    o16_ref[...] = y.astype(o16_ref.dtype)


def _moe_combine(dest, x32, gates, y_sorted, gam, bet, alpha):
    Np, D = x32.shape
    tc = _pick_tile(Np, 192, 16)
    row = lambda i, d: (i, 0)
    grid_spec = pltpu.PrefetchScalarGridSpec(
        num_scalar_prefetch=1, grid=(Np // tc,),
        in_specs=[
            pl.BlockSpec((tc, D), row),
            pl.BlockSpec((tc, LANE), row),
            pl.BlockSpec((1, D), lambda i, d: (0, 0)),
            pl.BlockSpec((1, D), lambda i, d: (0, 0)),
            pl.BlockSpec(memory_space=pl.ANY),
        ],
        out_specs=[pl.BlockSpec((tc, D), row), pl.BlockSpec((tc, D), row)],
        scratch_shapes=[pltpu.VMEM((EXPERT_TOPK, tc, D), jnp.float32), pltpu.SemaphoreType.DMA(())])
    return pl.pallas_call(
        functools.partial(_combine_kernel, tc=tc, alpha=alpha), grid_spec=grid_spec,
        out_shape=[jax.ShapeDtypeStruct((Np, D), jnp.float32), jax.ShapeDtypeStruct((Np, D), MXU_DTYPE)],
        compiler_params=_cparams("arbitrary"), name="moe_combine",
    )(dest, x32, gates, gam.reshape(1, D), bet.reshape(1, D), y_sorted)


def _rope_tables(pos, scale_wi):
    def cs(dim):
        half = dim // 2
        inv = ROPE_THETA ** (-jnp.arange(half, dtype=jnp.float32) * 2.0 / dim)
        ang = pos.astype(jnp.float32)[:, None] * inv[None, :]
        return jnp.cos(ang), jnp.sin(ang)
    c, s = cs(HEAD_DIM)
    t128 = (jnp.concatenate([c, c], -1), jnp.concatenate([-s, s], -1))
    c, s = cs(IDX_DIM)
    c64 = jnp.concatenate([c, c], -1)
    s64 = jnp.concatenate([-s, s], -1)
    t64 = (jnp.tile(c64, (1, LANE // IDX_DIM)), jnp.tile(s64, (1, LANE // IDX_DIM)))
    n = pos.shape[0]
    pad = LANE - IDX_DIM - IDX_HEADS
    tkw = (jnp.concatenate([c64, jnp.full((n, IDX_HEADS), scale_wi, jnp.float32), jnp.zeros((n, pad), jnp.float32)], -1),
           jnp.concatenate([s64, jnp.zeros((n, LANE - IDX_DIM), jnp.float32)], -1))
    return t128, t64, tkw


def kernel(x_prompt, x_sample, cache_a_k, cache_a_v, cache_a_kidx, cache_b_k, cache_b_v, page_table, p_prompt, p_sample, ln_emb_g, ln_emb_b, w_in, w_out, ln1_g, ln1_b, w_route_group, w_route_expert, w_exp_gate, w_exp_up, w_exp_down, ln2_g, ln2_b, w_ple, w_ple_gate, ln3_g, ln3_b):
    B, T, D = x_prompt.shape
    Bd, Td, _ = x_sample.shape
    depth = w_in.shape[0]
    n_phys, page = cache_a_k.shape[1], cache_a_k.shape[2]
    past = page_table.shape[1] * page
    n_experts = w_exp_gate.shape[1]
    h_a = D // 2 // HEAD_DIM
    h_b = h_a
    rep_a, rep_b = h_a // KV_A, h_b // KV_B
    assert IDX_DIM * 2 == LANE and IDX_DIM + IDX_HEADS <= LANE and page == LANE
    assert N_GROUPS * (1 + EXPERTS_PER_GROUP) <= LANE and n_experts == N_GROUPS * EXPERTS_PER_GROUP
    alpha = (2.0 * depth) ** 0.25
    n_p, n_s = B * T, Bd * Td
    N = n_p + n_s
    Np = _round_up(N, ROW_ALIGN)
    f32 = jnp.float32

    def stream(a_p, a_s):
        w = a_p.shape[-1]
        return jnp.concatenate([a_p.reshape(n_p, w), a_s.reshape(n_s, w), jnp.zeros((Np - N, w), a_p.dtype)], 0)

    pos = jnp.concatenate([jnp.tile(jnp.arange(T, dtype=jnp.int32), B),
                           jnp.tile(past + jnp.arange(Td, dtype=jnp.int32), Bd),
                           jnp.zeros((Np - N,), jnp.int32)])
    t128, t64, tkw = _rope_tables(pos, IDX_HEADS ** -0.5 * IDX_DIM ** -0.5)

    widths = (h_a * HEAD_DIM, KV_A * HEAD_DIM, KV_A * HEAD_DIM, IDX_HEADS * IDX_DIM, IDX_DIM, IDX_HEADS,
              h_b * HEAD_DIM, KV_B * HEAD_DIM, KV_B * HEAD_DIM)
    offs = [0]
    for w in widths:
        offs.append(offs[-1] + w)
    assert offs[-1] == w_in.shape[2]

    def cols(w, *ids):
        return jnp.concatenate([w[:, offs[i]:offs[i + 1]] for i in ids], axis=1).astype(MXU_DTYPE)

    kv_w = KV_A * HEAD_DIM
    trow = jnp.arange(Td, dtype=jnp.int32)[:, None]
    own_bias = jnp.where(jnp.arange(LANE, dtype=jnp.int32)[None, :] <= trow, 0.0, NEG_BIAS).astype(f32)[None]

    def fresh(rows):
        w = rows.shape[-1]
        return jnp.pad(rows.reshape(Bd, Td, w), ((0, 0), (0, LANE - Td), (0, 0)))

    def decode_q(q_sm, kv, rep):
        q = q_sm.reshape(kv, rep, Bd, Td, HEAD_DIM)
        return jnp.transpose(q, (2, 0, 1, 3, 4)).reshape(Bd, kv, rep * Td, HEAD_DIM)

    def decode_o(o, kv, rep):
        o = o.reshape(Bd, kv, rep, Td, HEAD_DIM)
        return jnp.transpose(o, (0, 3, 1, 2, 4)).reshape(n_s, kv * rep * HEAD_DIM)

    x_tail = jnp.concatenate([x_sample.reshape(n_s, D), jnp.zeros((Np - N, D), x_sample.dtype)], axis=0)
    h32, h16 = _layer_norm_embed(x_prompt.reshape(n_p, D), x_tail, ln_emb_g, ln_emb_b)
    rows_out = []
    for l in range(depth):
        wl = w_in[l]
        q128 = _matmul([h16], [cols(wl, 0, 6)], MXU_DTYPE, "rope128", t128, True, name="proj_q")
        k128 = _matmul([h16], [cols(wl, 1, 7)], f32, "rope128", t128, name="proj_k")
        v128 = _matmul([h16], [cols(wl, 2, 8)], f32, name="proj_v")
        qi = _matmul([h16], [cols(wl, 3)], MXU_DTYPE, "rope64", t64, True, name="proj_qi")
        w_kw = jnp.pad(cols(wl, 4, 5), ((0, 0), (0, LANE - IDX_DIM - IDX_HEADS)))
        kiwi = _matmul([h16], [w_kw], f32, "rope64", tkw, name="proj_kiwi")

        attn_a = _dsa_prompt(q128, k128, v128, qi, kiwi, B, T, rep_a)
        attn_b = _moba_prompt(q128, k128, v128, B, T, rep_b, h_a)

        ks, vs = k128[n_p:N], v128[n_p:N]
        kiwi_s = kiwi[n_p:N]
        qi_s = qi[:, n_p:N].reshape(IDX_HEADS // 2, Bd, Td, 2, IDX_DIM)
        qi_s = jnp.transpose(qi_s, (1, 0, 3, 2, 4)).reshape(Bd, IDX_HEADS * Td, IDX_DIM)
        wcol = jnp.transpose(kiwi_s[:, IDX_DIM:IDX_DIM + IDX_HEADS].reshape(Bd, Td, IDX_HEADS), (0, 2, 1))
        wcol = wcol.reshape(Bd, IDX_HEADS * Td, 1)
        bias_p, bias_n = _dsa_decode_mask(page_table, qi_s, wcol, cache_a_kidx, fresh(kiwi_s[:, :IDX_DIM]), l, Td)
        o_a = _paged_attention(page_table, decode_q(q128[:h_a, n_p:N], KV_A, rep_a), cache_a_k, cache_a_v,
                               fresh(ks[:, :kv_w]), fresh(vs[:, :kv_w]), bias_p[:, :, None], bias_n, l, Td, rep_a)
        qb_s = decode_q(q128[h_a:, n_p:N], KV_B, rep_b)
        bias_b = _moba_decode_mask(page_table, qb_s, cache_b_k, l, Td, rep_b)
        o_b = _paged_attention(page_table, qb_s, cache_b_k, cache_b_v, fresh(ks[:, kv_w:]), fresh(vs[:, kv_w:]),
                               bias_b, own_bias, l, Td, rep_b)
        tail = jnp.zeros((Np - N, h_a * HEAD_DIM), MXU_DTYPE)
        attn_a = jnp.concatenate([attn_a, decode_o(o_a, KV_A, rep_a).astype(MXU_DTYPE), tail], axis=0)
        attn_b = jnp.concatenate([attn_b, decode_o(o_b, KV_B, rep_b).astype(MXU_DTYPE), tail], axis=0)

        w_o = w_out[l].astype(MXU_DTYPE)
        mix = _matmul([attn_a, attn_b], [w_o[:h_a * HEAD_DIM], w_o[h_a * HEAD_DIM:]], f32, name="proj_out")
        w_r = jnp.pad(jnp.concatenate([w_route_group[l], w_route_expert[l]], axis=1),
                      ((0, 0), (0, LANE - N_GROUPS - n_experts)))
        w_rh = w_r.astype(MXU_DTYPE)
        w_rl = (w_r - w_rh.astype(f32)).astype(MXU_DTYPE)
        x32, x16, logits = _layer_norm([h32, mix], ln1_g[l], ln1_b[l], "add", alpha, (w_rh, w_rl))
        del x16

        eid, gates = _route(logits)
        plan = _moe_plan(eid[:N, :EXPERT_TOPK].reshape(-1), n_experts, MOE_ROWS, MOE_CHUNKS)
        y_sorted = _moe_experts(x32, plan, w_exp_gate, w_exp_up, w_exp_down, l)
        dest = jnp.pad(plan[1], (0, (Np - N) * EXPERT_TOPK))
        x32, x16 = _moe_combine(dest, x32, gates, y_sorted, ln2_g[l], ln2_b[l], alpha)

        gate_pre = _matmul([x16], [w_ple_gate[l].astype(MXU_DTYPE)], f32, name="ple_gate")
        p16 = stream(p_prompt[l], p_sample[l]).astype(MXU_DTYPE)
        ple = _matmul([p16], [w_ple[l].astype(MXU_DTYPE)], f32, name="ple_embed")
        h32, h16 = _layer_norm([x32, gate_pre, ple], ln3_g[l], ln3_b[l], "ple", alpha)
        rows_out.append((k128, v128, kiwi))

    def gather_rows(sel, lo, hi, lead):
        return jnp.stack([sel(r)[lo:hi].reshape(lead) for r in rows_out])

    outs = [h32[:n_p].reshape(B, T, D), h32[n_p:N].reshape(Bd, Td, D)]
    for lo, hi, lead in ((0, n_p, (B, T)), (n_p, N, (Bd, Td))):
        outs += [
            gather_rows(lambda r: r[0][:, :kv_w], lo, hi, lead + (KV_A, HEAD_DIM)),
            gather_rows(lambda r: r[1][:, :kv_w], lo, hi, lead + (KV_A, HEAD_DIM)),
            gather_rows(lambda r: r[2][:, :IDX_DIM], lo, hi, lead + (IDX_DIM,)),
            gather_rows(lambda r: r[0][:, kv_w:], lo, hi, lead + (KV_B, HEAD_DIM)),
            gather_rows(lambda r: r[1][:, kv_w:], lo, hi, lead + (KV_B, HEAD_DIM)),
        ]
    return tuple(outs)
```

```python
import functools

import jax
import jax.numpy as jnp
from jax import lax
from jax.experimental import pallas as pl
from jax.experimental.pallas import tpu as pltpu

HEAD_DIM = 128
KV_A = 4
IDX_HEADS = 32
IDX_DIM = 64
DSA_TOPK = 256
KV_B = 4
MOBA_BLOCK = 256
MOBA_TOPK = 3
N_GROUPS = 4
EXPERTS_PER_GROUP = 8
EXPERT_TOPK = 2
ROPE_THETA = 10000.0
LN_EPS = 1e-5

LANE = 128
ROW_ALIGN = 256
MXU_DTYPE = jnp.bfloat16
NEG_BIAS = -1e30
INT_MIN = -2147483648
VMEM_LIMIT = 56 * 1024 * 1024
PAGES_PER_CHUNK = 16
MOE_ROWS = 128
MOE_CHUNKS = 5
MOE_KT = 1024
MOE_NT = 1024


def _cparams(*sem):
    return pltpu.CompilerParams(dimension_semantics=sem, vmem_limit_bytes=VMEM_LIMIT)


def _round_up(n, m):
    return (n + m - 1) // m * m


def _pick_tile(n, cap, mult):
    best = None
    for t in range(mult, cap + 1, mult):
        if n % t == 0:
            best = t
    assert best is not None, (n, cap, mult)
    return best


def _dot_nt(a, b):
    return lax.dot_general(a, b, (((1,), (1,)), ((), ())), preferred_element_type=jnp.float32)


def _ln_kernel(*refs, mode, alpha, router, pack_kt):
    it = iter(refs)
    x_ref = next(it)
    a_ref = next(it) if mode == "add" else None
    g_ref = next(it) if mode == "ple" else None
    p_ref = next(it) if mode == "ple" else None
    gam_ref, bet_ref = next(it), next(it)
    wh_ref = next(it) if router else None
    wl_ref = next(it) if router else None
    o32_ref, o16_ref = next(it), next(it)
    lg_ref = next(it) if router else None

    v = x_ref[...]
    if mode == "add":
        v = alpha * v + a_ref[...]
    elif mode == "ple":
        v = alpha * v + jax.nn.sigmoid(g_ref[...]) * p_ref[...]
    mu = jnp.mean(v, axis=-1, keepdims=True)
    d = v - mu
    var = jnp.mean(d * d, axis=-1, keepdims=True)
    y = d * lax.rsqrt(var + LN_EPS) * gam_ref[...] + bet_ref[...]
    o32_ref[...] = y
    if pack_kt is not None:
        kt = pack_kt
        for b in range(o16_ref.shape[1] // kt):
            lo = y[:, 2 * b * kt:(2 * b + 1) * kt].astype(jnp.bfloat16).astype(jnp.float32)
            hi = y[:, (2 * b + 1) * kt:(2 * b + 2) * kt].astype(jnp.bfloat16).astype(jnp.float32)
            o16_ref[:, b * kt:(b + 1) * kt] = (
                lax.shift_right_logical(lax.bitcast_convert_type(lo, jnp.uint32), jnp.uint32(16))
                | (lax.bitcast_convert_type(hi, jnp.uint32) & jnp.uint32(0xFFFF0000)))
    else:
        o16_ref[...] = y.astype(o16_ref.dtype)
    if router:
        yh = y.astype(MXU_DTYPE)
        yl = (y - yh.astype(jnp.float32)).astype(MXU_DTYPE)
        wh = wh_ref[...]
        lg = jnp.dot(yh, wh, preferred_element_type=jnp.float32)
        lg = lg + jnp.dot(yl, wh, preferred_element_type=jnp.float32)
        lg = lg + jnp.dot(yh, wl_ref[...], preferred_element_type=jnp.float32)
        lg_ref[...] = lg


def _ln_embed_kernel(xh_ref, xt_ref, gam_ref, bet_ref, o32_ref, o16_ref, *, n_head):
    v = jnp.where(pl.program_id(0) < n_head, xh_ref[...], xt_ref[...])
    mu = jnp.mean(v, axis=-1, keepdims=True)
    d = v - mu
    var = jnp.mean(d * d, axis=-1, keepdims=True)
    y = d * lax.rsqrt(var + LN_EPS) * gam_ref[...] + bet_ref[...]
    o32_ref[...] = y
    o16_ref[...] = y.astype(o16_ref.dtype)


def _layer_norm_embed(x_head, x_tail, gam, bet):
    n_h, D = x_head.shape
    n_t = x_tail.shape[0]
    tr = _pick_tile(ROW_ALIGN, 128, 16)
    assert n_h % tr == 0 and n_t % tr == 0
    nh, ntl = n_h // tr, n_t // tr
    vec = pl.BlockSpec((1, D), lambda i: (0, 0))
    row = pl.BlockSpec((tr, D), lambda i: (i, 0))
    return pl.pallas_call(
        functools.partial(_ln_embed_kernel, n_head=nh), grid=(nh + ntl,),
        in_specs=[pl.BlockSpec((tr, D), lambda i: (jnp.minimum(i, nh - 1), 0)),
                  pl.BlockSpec((tr, D), lambda i: (jnp.maximum(i - nh, 0), 0)), vec, vec],
        out_specs=[row, row],
        out_shape=[jax.ShapeDtypeStruct((n_h + n_t, D), jnp.float32),
                   jax.ShapeDtypeStruct((n_h + n_t, D), MXU_DTYPE)],
        compiler_params=_cparams("parallel"), name="ln_embed")(x_head, x_tail, gam.reshape(1, D), bet.reshape(1, D))


def _layer_norm(xs, gam, bet, mode, alpha=1.0, router_w=None, pack_kt=None):
    Np, D = xs[0].shape
    tr = _pick_tile(Np, 192, 16)
    row = pl.BlockSpec((tr, D), lambda i: (i, 0))
    vec = pl.BlockSpec((1, D), lambda i: (0, 0))
    in_specs = [row] * len(xs) + [vec, vec]
    args = list(xs) + [gam.reshape(1, D), bet.reshape(1, D)]
    if pack_kt is None:
        out_shape = [jax.ShapeDtypeStruct((Np, D), jnp.float32), jax.ShapeDtypeStruct((Np, D), MXU_DTYPE)]
        out_specs = [row, row]
    else:
        assert D % (2 * pack_kt) == 0
        out_shape = [jax.ShapeDtypeStruct((Np, D), jnp.float32), jax.ShapeDtypeStruct((Np, D // 2), jnp.uint32)]
        out_specs = [row, pl.BlockSpec((tr, D // 2), lambda i: (i, 0))]
    if router_w is not None:
        wspec = pl.BlockSpec((D, LANE), lambda i: (0, 0))
        in_specs += [wspec, wspec]
        args += list(router_w)
        out_shape.append(jax.ShapeDtypeStruct((Np, LANE), jnp.float32))
        out_specs.append(pl.BlockSpec((tr, LANE), lambda i: (i, 0)))
    return pl.pallas_call(
        functools.partial(_ln_kernel, mode=mode, alpha=alpha, router=router_w is not None, pack_kt=pack_kt),
        grid=(Np // tr,), in_specs=in_specs, out_specs=out_specs, out_shape=out_shape,
        compiler_params=_cparams("parallel"), name="ln_" + mode)(*args)


def _mm_kernel(*refs, n_x, mode, slice_major):
    x_refs = refs[:n_x]
    w_refs = refs[n_x:2 * n_x]
    rest = refs[2 * n_x:]
    if mode == "none":
        (o_ref,) = rest
    else:
        cos_ref, sin_ref, o_ref = rest
    y = jnp.dot(x_refs[0][...], w_refs[0][...], preferred_element_type=jnp.float32)
    for k in range(1, n_x):
        y = y + jnp.dot(x_refs[k][...], w_refs[k][...], preferred_element_type=jnp.float32)
    if mode == "none" and not slice_major:
        o_ref[...] = y.astype(o_ref.dtype)
        return
    tm = y.shape[0]
    if mode != "none":
        cos = cos_ref[...]
        sin = sin_ref[...]
    if mode == "rope64":
        lane = lax.broadcasted_iota(jnp.int32, (tm, LANE), 1)
        first_half = (lane % 64) < 32
    for s in range(y.shape[1] // LANE):
        yh = y[:, s * LANE:(s + 1) * LANE]
        if mode == "rope128":
            yh = yh * cos + pltpu.roll(yh, 64, 1) * sin
        elif mode == "rope64":
            partner = jnp.where(first_half, pltpu.roll(yh, 96, 1), pltpu.roll(yh, 32, 1))
            yh = yh * cos + partner * sin
        if slice_major:
            o_ref[s] = yh.astype(o_ref.dtype)
        else:
            o_ref[:, s * LANE:(s + 1) * LANE] = yh.astype(o_ref.dtype)


def _matmul(xs, ws, out_dtype, mode="none", tables=None, slice_major=False, name="mm"):
    Np = xs[0].shape[0]
    Nc = ws[0].shape[1]
    tm = _pick_tile(Np, 640, 16)
    tn = _pick_tile(Nc, 1024, LANE)
    in_specs = [pl.BlockSpec((tm, x.shape[1]), lambda j, i: (i, 0)) for x in xs]
    in_specs += [pl.BlockSpec((w.shape[0], tn), lambda j, i: (0, j)) for w in ws]
    args = list(xs) + list(ws)
    if mode != "none":
        in_specs += [pl.BlockSpec((tm, LANE), lambda j, i: (i, 0))] * 2
        args += list(tables)
    if slice_major:
        out_shape = jax.ShapeDtypeStruct((Nc // LANE, Np, LANE), out_dtype)
        out_spec = pl.BlockSpec((tn // LANE, tm, LANE), lambda j, i: (j, i, 0))
    else:
        out_shape = jax.ShapeDtypeStruct((Np, Nc), out_dtype)
        out_spec = pl.BlockSpec((tm, tn), lambda j, i: (i, j))
    return pl.pallas_call(
        functools.partial(_mm_kernel, n_x=len(xs), mode=mode, slice_major=slice_major),
        grid=(Nc // tn, Np // tm), in_specs=in_specs, out_specs=out_spec, out_shape=out_shape,
        compiler_params=_cparams("parallel", "parallel"), name=name)(*args)


def _float_keys(x):
    b = lax.bitcast_convert_type(x, jnp.int32)
    return jnp.where(b < 0, b ^ jnp.int32(0x7FFFFFFF), b)


def _kth_largest(count_ge, shape, k):
    def body(it, ans):
        cand = ans + lax.shift_left(jnp.int32(1), 31 - it)
        return jnp.where(count_ge(cand) >= k, cand, ans)
    return lax.fori_loop(0, 32, body, jnp.full(shape, INT_MIN, jnp.int32))


def _topk_lanes(gate, k):
    lane = lax.broadcasted_iota(jnp.int32, gate.shape, 1)
    sel = jnp.zeros(gate.shape, jnp.float32)
    g = gate
    for _ in range(k):
        m = jnp.max(g, axis=-1, keepdims=True)
        idx = jnp.min(jnp.where(g == m, lane, LANE), axis=-1, keepdims=True)
        hit = lane == idx
        sel = jnp.where(hit, 1.0, sel)
        g = jnp.where(hit, -jnp.inf, g)
    return sel


def _attend(q_ref, k_ref, v_ref, bias_ref, o_ref, rep, scale, width):
    k = k_ref[:width, :].astype(MXU_DTYPE)
    v = v_ref[:width, :].astype(MXU_DTYPE)
    bias = bias_ref[:, :width]
    for r in range(rep):
        s = _dot_nt(q_ref[r], k) * scale + bias
        m = jnp.max(s, axis=-1, keepdims=True)
        p = jnp.exp(s - m)
        l = jnp.sum(p, axis=-1, keepdims=True)
        o = jnp.dot(p.astype(MXU_DTYPE), v, preferred_element_type=jnp.float32) / l
        o_ref[:, r * HEAD_DIM:(r + 1) * HEAD_DIM] = o.astype(o_ref.dtype)


def _causal_extents(n_tiles, max_branches=4):
    nbr = min(max_branches, n_tiles)
    out, lo = [], 0
    for hi in sorted({-(-n_tiles * (b + 1) // nbr) for b in range(nbr)}):
        out.append((lo, hi))
        lo = hi
    return out


def _dsa_select(qi_ref, wq_ref, kiw_ref, bias_ref, key_ref, i, tq, W, n_keep):
    klane = lax.broadcasted_iota(jnp.int32, (W, LANE), 1)
    ka32 = jnp.where(klane < IDX_DIM, kiw_ref[:W, :], 0.0)
    ka = ka32.astype(MXU_DTYPE)
    kb = pltpu.roll(ka32, IDX_DIM, 1).astype(MXU_DTYPE)
    wq = wq_ref[...]
    score = None
    for hp in range(IDX_HEADS // 2):
        qp = qi_ref[hp]
        c = (jnp.maximum(_dot_nt(qp, ka), 0.0) * wq[:, IDX_DIM + 2 * hp:IDX_DIM + 2 * hp + 1]
             + jnp.maximum(_dot_nt(qp, kb), 0.0) * wq[:, IDX_DIM + 2 * hp + 1:IDX_DIM + 2 * hp + 2])
        score = c if score is None else score + c
    row = lax.broadcasted_iota(jnp.int32, (tq, W), 0) + i * tq
    col = lax.broadcasted_iota(jnp.int32, (tq, W), 1)
    causal = col <= row
    key_ref[:, :W] = jnp.where(causal, _float_keys(score), INT_MIN)

    def count_ge(cand):
        return jnp.sum(jnp.where(key_ref[:, :W] >= cand, 1, 0), axis=-1, keepdims=True)

    thr = _kth_largest(count_ge, (tq, 1), n_keep)
    keys = key_ref[:, :W]
    gt = keys > thr
    eq = keys == thr
    n_gt = jnp.sum(jnp.where(gt, 1, 0), axis=-1, keepdims=True)
    n_eq = jnp.sum(jnp.where(eq, 1, 0), axis=-1, keepdims=True)
    bias_ref[:, :W] = jnp.where((gt | eq) & causal, 0.0, NEG_BIAS)
    tie = (thr > INT_MIN) & (n_gt + n_eq > n_keep)

    @pl.when(jnp.max(jnp.where(tie, 1, 0)) > 0)
    def _():
        need = n_keep - n_gt
        nbits = W.bit_length()

        def body(it, lim):
            cand = lim + lax.shift_left(jnp.int32(1), nbits - 1 - it)
            c = jnp.sum(jnp.where(eq & (col < cand), 1, 0), axis=-1, keepdims=True)
            return jnp.where(c <= need, cand, lim)

        lim = lax.fori_loop(0, nbits, body, jnp.zeros((tq, 1), jnp.int32))
        bias_ref[:, :W] = jnp.where((gt | (eq & (col < lim))) & causal, 0.0, NEG_BIAS)


def _dsa_prompt_kernel(q_ref, k_ref, v_ref, qi_ref, wq_ref, kiw_ref, o_ref, bias_ref, key_ref,
                       *, tq, T, rep, n_keep, scale):
    i = pl.program_id(1)
    g = pl.program_id(2)
    for lo, hi in _causal_extents(T // tq):
        @pl.when((i >= lo) & (i < hi))
        def _(W=hi * tq):
            @pl.when(g == 0)
            def _():
                _dsa_select(qi_ref, wq_ref, kiw_ref, bias_ref, key_ref, i, tq, W, n_keep)

            _attend(q_ref, k_ref, v_ref, bias_ref, o_ref, rep, scale, W)


def _dsa_prompt(q128, k128, v128, qi, kiwi, B, T, rep):
    tq = min(256, T)
    assert T % tq == 0 and tq % LANE == 0
    nT = T // tq
    n_keep = min(DSA_TOPK, T // 4)
    kern = functools.partial(_dsa_prompt_kernel, tq=tq, T=T, rep=rep, n_keep=n_keep,
                             scale=HEAD_DIM ** -0.5)
    return pl.pallas_call(
        kern, grid=(B, nT, KV_A),
        in_specs=[
            pl.BlockSpec((rep, tq, HEAD_DIM), lambda b, i, g: (g, b * nT + i, 0)),
            pl.BlockSpec((T, HEAD_DIM), lambda b, i, g: (b, g)),
            pl.BlockSpec((T, HEAD_DIM), lambda b, i, g: (b, g)),
            pl.BlockSpec((IDX_HEADS // 2, tq, LANE), lambda b, i, g: (0, b * nT + i, 0)),
            pl.BlockSpec((tq, LANE), lambda b, i, g: (b * nT + i, 0)),
            pl.BlockSpec((T, LANE), lambda b, i, g: (b, 0)),
        ],
        out_specs=pl.BlockSpec((tq, rep * HEAD_DIM), lambda b, i, g: (b * nT + i, g)),
        out_shape=jax.ShapeDtypeStruct((B * T, KV_A * rep * HEAD_DIM), MXU_DTYPE),
        scratch_shapes=[pltpu.VMEM((tq, T), jnp.float32), pltpu.VMEM((tq, T), jnp.int32)],
        compiler_params=_cparams("parallel", "parallel", "arbitrary"), name="dsa_prompt",
    )(q128, k128, v128, qi, kiwi, kiwi)


def _moba_prompt_kernel(q_ref, k_ref, v_ref, o_ref, bias_ref, *, bs, nblk, rep, n_pick, scale):
    i = pl.program_id(1)
    qsum = q_ref[0].astype(jnp.float32)
    for r in range(1, rep):
        qsum = qsum + q_ref[r].astype(jnp.float32)
    lane = lax.broadcasted_iota(jnp.int32, (bs, LANE), 1)
    past = lane < i
    row = lax.broadcasted_iota(jnp.int32, (bs, bs), 0)
    col = lax.broadcasted_iota(jnp.int32, (bs, bs), 1)
    own_bias = jnp.where(col <= row, 0.0, NEG_BIAS)
    for lo, hi in _causal_extents(nblk):
        @pl.when((i >= lo) & (i < hi))
        def _(nb=hi):
            gate = jnp.full((bs, LANE), -jnp.inf, jnp.float32)
            for n in range(nb):
                mean_n = jnp.mean(k_ref[n * bs:(n + 1) * bs, :], axis=0, keepdims=True)
                gate = jnp.where(lane == n, jnp.sum(qsum * mean_n, axis=-1, keepdims=True), gate)
            sel = jnp.where(past, _topk_lanes(jnp.where(past, gate, -jnp.inf), n_pick), 0.0)
            for n in range(nb):
                picked = jnp.where(sel[:, n:n + 1] > 0.5, 0.0, NEG_BIAS)
                bias_ref[:, n * bs:(n + 1) * bs] = jnp.where(i == n, own_bias, jnp.broadcast_to(picked, (bs, bs)))
            _attend(q_ref, k_ref, v_ref, bias_ref, o_ref, rep, scale, nb * bs)


def _moba_prompt(q128, k128, v128, B, T, rep, h_a):
    bs = MOBA_BLOCK
    assert T % bs == 0
    nblk = T // bs
    assert 1 <= nblk <= LANE
    kern = functools.partial(_moba_prompt_kernel, bs=bs, nblk=nblk, rep=rep,
                             n_pick=min(MOBA_TOPK, nblk), scale=HEAD_DIM ** -0.5)
    return pl.pallas_call(
        kern, grid=(B, nblk, KV_B),
        in_specs=[
            pl.BlockSpec((rep, bs, HEAD_DIM), lambda b, i, g: (h_a // rep + g, b * nblk + i, 0)),
            pl.BlockSpec((T, HEAD_DIM), lambda b, i, g: (b, KV_A + g)),
            pl.BlockSpec((T, HEAD_DIM), lambda b, i, g: (b, KV_A + g)),
        ],
        out_specs=pl.BlockSpec((bs, rep * HEAD_DIM), lambda b, i, g: (b * nblk + i, g)),
        out_shape=jax.ShapeDtypeStruct((B * T, KV_B * rep * HEAD_DIM), MXU_DTYPE),
        scratch_shapes=[pltpu.VMEM((bs, T), jnp.float32)],
        compiler_params=_cparams("parallel", "parallel", "parallel"), name="moba_prompt",
    )(q128, k128, v128)


def _page_copies(pt_ref, pool_ref, layer, buf, sem, b, chunk, slot, pages, page, groups=0):
    out = []
    for p in range(pages):
        phys = pt_ref[b, chunk * pages + p]
        rows = pl.ds(p * page, page)
        if not groups:
            out.append(pltpu.make_async_copy(pool_ref.at[layer, phys], buf.at[slot, rows], sem.at[slot]))
        for g in range(groups):
            out.append(pltpu.make_async_copy(pool_ref.at[layer, phys, :, g, :], buf.at[slot, g, rows], sem.at[slot]))
    return out


def _paged_step(c, nchunk, start_fn, wait_fn):
    @pl.when(c == 0)
    def _():
        start_fn(0, 0)

    slot = c % 2
    wait_fn(c, slot)

    @pl.when(c + 1 < nchunk)
    def _():
        start_fn(c + 1, 1 - slot)

    return slot


def _dsa_decode_mask_kernel(pt_ref, qi_ref, wc_ref, pool_ref, knew_ref, bp_ref, bn_ref,
                            kbuf, sem, sp_ref, sn_ref, *, layer, nchunk, pages, page, Td, n_keep, past):
    b = pl.program_id(0)
    c = pl.program_id(1)
    CH = pages * page

    def start_fn(cc, slot):
        for cp in _page_copies(pt_ref, pool_ref, layer, kbuf, sem, b, cc, slot, pages, page):
            cp.start()

    def wait_fn(cc, slot):
        for cp in _page_copies(pt_ref, pool_ref, layer, kbuf, sem, b, cc, slot, pages, page):
            cp.wait()

    def scores(kc, width):
        d = _dot_nt(qi_ref[0], kc.astype(MXU_DTYPE))
        r = jnp.maximum(d, 0.0) * wc_ref[0]
        return jnp.sum(r.reshape(IDX_HEADS, Td, width), axis=0)

    @pl.when(c < nchunk)
    def _():
        slot = _paged_step(c, nchunk, start_fn, wait_fn)
        sp_ref[c] = scores(kbuf[slot], CH)

    @pl.when(c == nchunk)
    def _():
        sn_ref[...] = scores(knew_ref[0], LANE)
        trow = lax.broadcasted_iota(jnp.int32, (Td, LANE), 0)
        ncol = lax.broadcasted_iota(jnp.int32, (Td, LANE), 1)
        new_ok = ncol <= trow
        kp = _float_keys(sp_ref[...])
        kn = jnp.where(new_ok, _float_keys(sn_ref[...]), INT_MIN)

        def count(mp, mn):
            return (jnp.sum(jnp.sum(jnp.where(mp, 1, 0), axis=0), axis=-1, keepdims=True)
                    + jnp.sum(jnp.where(mn, 1, 0), axis=-1, keepdims=True))

        thr = _kth_largest(lambda cand: count(kp >= cand[None], kn >= cand), (Td, 1), n_keep)
        gt_p, eq_p = kp > thr[None], kp == thr[None]
        gt_n, eq_n = kn > thr, kn == thr
        n_gt = count(gt_p, gt_n)
        n_eq = count(eq_p, eq_n)
        bp_ref[0] = jnp.where(gt_p | eq_p, 0.0, NEG_BIAS)
        bn_ref[0] = jnp.where((gt_n | eq_n) & new_ok, 0.0, NEG_BIAS)
        tie = (thr > INT_MIN) & (n_gt + n_eq > n_keep)

        @pl.when(jnp.max(jnp.where(tie, 1, 0)) > 0)
        def _():
            need = n_keep - n_gt
            pos_p = (lax.broadcasted_iota(jnp.int32, (nchunk, Td, CH), 0) * CH
                     + lax.broadcasted_iota(jnp.int32, (nchunk, Td, CH), 2))
            pos_n = past + ncol
            nbits = (past + LANE).bit_length()

            def body(it, lim):
                cand = lim + lax.shift_left(jnp.int32(1), nbits - 1 - it)
                cnt = count(eq_p & (pos_p < cand[None]), eq_n & (pos_n < cand))
                return jnp.where(cnt <= need, cand, lim)

            lim = lax.fori_loop(0, nbits, body, jnp.zeros((Td, 1), jnp.int32))
            bp_ref[0] = jnp.where(gt_p | (eq_p & (pos_p < lim[None])), 0.0, NEG_BIAS)
            bn_ref[0] = jnp.where((gt_n | (eq_n & (pos_n < lim))) & new_ok, 0.0, NEG_BIAS)


def _dsa_decode_mask(page_table, qi_s, wcol, pool, knew, layer, Td):
    Bd, n_pages = page_table.shape
    page = pool.shape[2]
    pages = min(PAGES_PER_CHUNK, n_pages)
    assert n_pages % pages == 0
    nchunk = n_pages // pages
    CH = pages * page
    past = n_pages * page
    n_keep = min(DSA_TOPK, (past + Td) // 4)
    HT = qi_s.shape[1]
    kern = functools.partial(_dsa_decode_mask_kernel, layer=layer, nchunk=nchunk, pages=pages, page=page,
                             Td=Td, n_keep=n_keep, past=past)
    grid_spec = pltpu.PrefetchScalarGridSpec(
        num_scalar_prefetch=1, grid=(Bd, nchunk + 1),
        in_specs=[
            pl.BlockSpec((1, HT, IDX_DIM), lambda b, c, pt: (b, 0, 0)),
            pl.BlockSpec((1, HT, 1), lambda b, c, pt: (b, 0, 0)),
            pl.BlockSpec(memory_space=pl.ANY),
            pl.BlockSpec((1, LANE, IDX_DIM), lambda b, c, pt: (b, 0, 0)),
        ],
        out_specs=[
            pl.BlockSpec((1, nchunk, Td, CH), lambda b, c, pt: (b, 0, 0, 0)),
            pl.BlockSpec((1, Td, LANE), lambda b, c, pt: (b, 0, 0)),
        ],
        scratch_shapes=[
            pltpu.VMEM((2, CH, IDX_DIM), jnp.float32),
            pltpu.SemaphoreType.DMA((2,)),
            pltpu.VMEM((nchunk, Td, CH), jnp.float32),
            pltpu.VMEM((Td, LANE), jnp.float32),
        ])
    return pl.pallas_call(
        kern, grid_spec=grid_spec,
        out_shape=[jax.ShapeDtypeStruct((Bd, nchunk, Td, CH), jnp.float32),
                   jax.ShapeDtypeStruct((Bd, Td, LANE), jnp.float32)],
        compiler_params=_cparams("arbitrary", "arbitrary"), name="dsa_decode_mask",
    )(page_table, qi_s, wcol, pool, knew)


def _moba_decode_mask_kernel(pt_ref, q_ref, pool_ref, bp_ref, kbuf, sem, gate_ref,
                             *, layer, nchunk, pages, page, Td, rep, bs, n_pick):
    b = pl.program_id(0)
    c = pl.program_id(1)
    CH = pages * page
    per = CH // bs
    nblk = nchunk * per

    def start_fn(cc, slot):
        for cp in _page_copies(pt_ref, pool_ref, layer, kbuf, sem, b, cc, slot, pages, page, KV_B):
            cp.start()

    def wait_fn(cc, slot):
        for cp in _page_copies(pt_ref, pool_ref, layer, kbuf, sem, b, cc, slot, pages, page, KV_B):
            cp.wait()

    @pl.when(c == 0)
    def _():
        gate_ref[...] = jnp.full(gate_ref.shape, -jnp.inf, jnp.float32)

    slot = _paged_step(c, nchunk, start_fn, wait_fn)
    lane = lax.broadcasted_iota(jnp.int32, (Td, LANE), 1)
    for g in range(KV_B):
        qsum = jnp.sum(q_ref[0, g].astype(jnp.float32).reshape(rep, Td, HEAD_DIM), axis=0)
        gate = gate_ref[g]
        for nb in range(per):
            mean_nb = jnp.mean(kbuf[slot, g, nb * bs:(nb + 1) * bs, :], axis=0, keepdims=True)
            gate = jnp.where(lane == c * per + nb, jnp.sum(qsum * mean_nb, axis=-1, keepdims=True), gate)
        gate_ref[g] = gate

    @pl.when(c == nchunk - 1)
    def _():
        for g in range(KV_B):
            sel = jnp.where(lane < nblk, _topk_lanes(gate_ref[g], n_pick), 0.0)
            for n in range(nblk):
                picked = jnp.where(sel[:, n:n + 1] > 0.5, 0.0, NEG_BIAS)
                bp_ref[0, n // per, g, :, (n % per) * bs:(n % per + 1) * bs] = jnp.broadcast_to(picked, (Td, bs))


def _moba_decode_mask(page_table, q_s, pool, layer, Td, rep):
    Bd, n_pages = page_table.shape
    page = pool.shape[2]
    pages = min(PAGES_PER_CHUNK, n_pages)
    assert n_pages % pages == 0
    nchunk = n_pages // pages
    CH = pages * page
    bs = MOBA_BLOCK
    assert CH % bs == 0 and Td <= bs
    nblk = n_pages * page // bs
    assert nblk <= LANE
    kern = functools.partial(_moba_decode_mask_kernel, layer=layer, nchunk=nchunk, pages=pages, page=page,
                             Td=Td, rep=rep, bs=bs, n_pick=min(MOBA_TOPK, (n_pages * page + Td) // bs))
    grid_spec = pltpu.PrefetchScalarGridSpec(
        num_scalar_prefetch=1, grid=(Bd, nchunk),
        in_specs=[
            pl.BlockSpec((1, KV_B, rep * Td, HEAD_DIM), lambda b, c, pt: (b, 0, 0, 0)),
            pl.BlockSpec(memory_space=pl.ANY),
        ],
        out_specs=pl.BlockSpec((1, nchunk, KV_B, Td, CH), lambda b, c, pt: (b, 0, 0, 0, 0)),
        scratch_shapes=[
            pltpu.VMEM((2, KV_B, CH, HEAD_DIM), jnp.float32),
            pltpu.SemaphoreType.DMA((2,)),
            pltpu.VMEM((KV_B, Td, LANE), jnp.float32),
        ])
    return pl.pallas_call(
        kern, grid_spec=grid_spec,
        out_shape=jax.ShapeDtypeStruct((Bd, nchunk, KV_B, Td, CH), jnp.float32),
        compiler_params=_cparams("arbitrary", "arbitrary"), name="moba_decode_mask",
    )(page_table, q_s, pool)


def _paged_attn_kernel(pt_ref, q_ref, kpool_ref, vpool_ref, knew_ref, vnew_ref, bp_ref, bn_ref, o_ref,
                       kbuf, vbuf, sem, m_ref, l_ref, acc_ref,
                       *, layer, nchunk, pages, page, Td, rep, n_kv, per_group_bias, scale):
    b = pl.program_id(0)
    c = pl.program_id(1)

    def start_fn(cc, slot):
        for cp in (_page_copies(pt_ref, kpool_ref, layer, kbuf, sem.at[0], b, cc, slot, pages, page, n_kv)
                   + _page_copies(pt_ref, vpool_ref, layer, vbuf, sem.at[1], b, cc, slot, pages, page, n_kv)):
            cp.start()

    def wait_fn(cc, slot):
        for cp in (_page_copies(pt_ref, kpool_ref, layer, kbuf, sem.at[0], b, cc, slot, pages, page, n_kv)
                   + _page_copies(pt_ref, vpool_ref, layer, vbuf, sem.at[1], b, cc, slot, pages, page, n_kv)):
            cp.wait()

    @pl.when(c == 0)
    def _():
        m_ref[...] = jnp.full(m_ref.shape, -jnp.inf, jnp.float32)
        l_ref[...] = jnp.zeros(l_ref.shape, jnp.float32)
        acc_ref[...] = jnp.zeros(acc_ref.shape, jnp.float32)

    def process(k_fn, v_fn, bias_fn):
        for g in range(n_kv):
            k = k_fn(g).astype(MXU_DTYPE)
            v = v_fn(g).astype(MXU_DTYPE)
            bias = bias_fn(g)
            s = _dot_nt(q_ref[0, g], k) * scale + jnp.concatenate([bias] * rep, axis=0)
            m_old = m_ref[g]
            m_new = jnp.maximum(m_old, jnp.max(s, axis=-1, keepdims=True))
            a = jnp.exp(m_old - m_new)
            p = jnp.exp(s - m_new)
            l_ref[g] = a * l_ref[g] + jnp.sum(p, axis=-1, keepdims=True)
            acc_ref[g] = a * acc_ref[g] + jnp.dot(p.astype(MXU_DTYPE), v, preferred_element_type=jnp.float32)
            m_ref[g] = m_new

    @pl.when(c < nchunk)
    def _():
        slot = _paged_step(c, nchunk, start_fn, wait_fn)
        process(lambda g: kbuf[slot, g], lambda g: vbuf[slot, g],
                lambda g: bp_ref[0, 0, g if per_group_bias else 0])

    @pl.when(c == nchunk)
    def _():
        process(lambda g: knew_ref[0, :, g * HEAD_DIM:(g + 1) * HEAD_DIM],
                lambda g: vnew_ref[0, :, g * HEAD_DIM:(g + 1) * HEAD_DIM], lambda g: bn_ref[0])
        for g in range(n_kv):
            o_ref[0, g] = acc_ref[g] / l_ref[g]


def _paged_attention(page_table, q_s, kpool, vpool, knew, vnew, bias_past, bias_new, layer, Td, rep):
    Bd, n_pages = page_table.shape
    page = kpool.shape[2]
    pages = min(PAGES_PER_CHUNK, n_pages)
    nchunk = n_pages // pages
    CH = pages * page
    n_kv = q_s.shape[1]
    R = rep * Td
    W = n_kv * HEAD_DIM
    gb = bias_past.shape[2]
    nb_new = bias_new.shape[0]
    kern = functools.partial(_paged_attn_kernel, layer=layer, nchunk=nchunk, pages=pages, page=page, Td=Td,
                             rep=rep, n_kv=n_kv, per_group_bias=gb > 1, scale=HEAD_DIM ** -0.5)
    grid_spec = pltpu.PrefetchScalarGridSpec(
        num_scalar_prefetch=1, grid=(Bd, nchunk + 1),
        in_specs=[
            pl.BlockSpec((1, n_kv, R, HEAD_DIM), lambda b, c, pt: (b, 0, 0, 0)),
            pl.BlockSpec(memory_space=pl.ANY),
            pl.BlockSpec(memory_space=pl.ANY),
            pl.BlockSpec((1, LANE, W), lambda b, c, pt: (b, 0, 0)),
            pl.BlockSpec((1, LANE, W), lambda b, c, pt: (b, 0, 0)),
            pl.BlockSpec((1, 1, gb, Td, CH), lambda b, c, pt: (b, jnp.minimum(c, nchunk - 1), 0, 0, 0)),
            pl.BlockSpec((1, Td, LANE), lambda b, c, pt: (b if nb_new > 1 else 0, 0, 0)),
        ],
        out_specs=pl.BlockSpec((1, n_kv, R, HEAD_DIM), lambda b, c, pt: (b, 0, 0, 0)),
        scratch_shapes=[
            pltpu.VMEM((2, n_kv, CH, HEAD_DIM), jnp.float32),
            pltpu.VMEM((2, n_kv, CH, HEAD_DIM), jnp.float32),
            pltpu.SemaphoreType.DMA((2, 2)),
            pltpu.VMEM((n_kv, R, 1), jnp.float32),
            pltpu.VMEM((n_kv, R, 1), jnp.float32),
            pltpu.VMEM((n_kv, R, HEAD_DIM), jnp.float32),
        ])
    return pl.pallas_call(
        kern, grid_spec=grid_spec,
        out_shape=jax.ShapeDtypeStruct((Bd, n_kv, R, HEAD_DIM), jnp.float32),
        compiler_params=_cparams("arbitrary", "arbitrary"), name="paged_attention",
    )(page_table, q_s, kpool, vpool, knew, vnew, bias_past, bias_new)


def _route_kernel(lg_ref, eid_ref, gate_ref):
    lg = lg_ref[...]
    lane = lax.broadcasted_iota(jnp.int32, lg.shape, 1)
    gmask = lane < N_GROUPS
    gl = jnp.where(gmask, lg, -jnp.inf)
    gmax = jnp.max(gl, axis=-1, keepdims=True)
    gsel = jnp.min(jnp.where(gl == gmax, lane, LANE), axis=-1, keepdims=True)
    g_w = 1.0 / jnp.sum(jnp.where(gmask, jnp.exp(gl - gmax), 0.0), axis=-1, keepdims=True)
    lo = N_GROUPS + gsel * EXPERTS_PER_GROUP
    el = jnp.where((lane >= lo) & (lane < lo + EXPERTS_PER_GROUP), lg, -jnp.inf)
    v1 = jnp.max(el, axis=-1, keepdims=True)
    i1 = jnp.min(jnp.where(el == v1, lane, LANE), axis=-1, keepdims=True)
    el2 = jnp.where(lane == i1, -jnp.inf, el)
    v2 = jnp.max(el2, axis=-1, keepdims=True)
    i2 = jnp.min(jnp.where(el2 == v2, lane, LANE), axis=-1, keepdims=True)
    e2 = jnp.exp(v2 - v1)
    p1 = 1.0 / (1.0 + e2)
    eid_ref[...] = jnp.where(lane == 0, i1 - N_GROUPS, jnp.where(lane == 1, i2 - N_GROUPS, 0))
    gate_ref[...] = jnp.where(lane == 0, g_w * p1, jnp.where(lane == 1, g_w * (e2 * p1), 0.0))


def _route(logits):
    Np = logits.shape[0]
    tr = _pick_tile(Np, 1024, 8)
    spec = pl.BlockSpec((tr, LANE), lambda i: (i, 0))
    return pl.pallas_call(
        _route_kernel, grid=(Np // tr,), in_specs=[spec], out_specs=[spec, spec],
        out_shape=[jax.ShapeDtypeStruct((Np, LANE), jnp.int32), jax.ShapeDtypeStruct((Np, LANE), jnp.float32)],
        compiler_params=_cparams("parallel"), name="route")(logits)


def _moe_plan(eid, n_experts, rc, nch_max):
    M = eid.shape[0]
    C = rc * nch_max
    order = jnp.argsort(eid, stable=True).astype(jnp.int32)
    eid_s = eid[order]
    counts = jnp.bincount(eid, length=n_experts).astype(jnp.int32)
    start = jnp.cumsum(counts) - counts
    pcounts = (counts + rc - 1) // rc * rc
    pstart = jnp.cumsum(pcounts) - pcounts
    slot_s = (pstart[eid_s] + jnp.arange(M, dtype=jnp.int32) - start[eid_s]).astype(jnp.int32)
    R = _round_up(M, rc) + n_experts * rc
    pend = pstart + pcounts
    slots = jnp.arange(R, dtype=jnp.int32)
    e_slot = jnp.minimum(jnp.sum(pend[None, :] <= slots[:, None], axis=1), n_experts - 1)
    j_slot = slots - pstart[e_slot]
    src = jnp.clip(start[e_slot] + j_slot, 0, M - 1)
    tok = jnp.where((j_slot < counts[e_slot]) & (slots < pend[-1]), order[src] // EXPERT_TOPK, 0)
    dest = slot_s[jnp.argsort(order)]
    nseg = (pcounts + C - 1) // C
    send = jnp.cumsum(nseg)
    sstart = send - nseg
    n_seg = -(-R // C) + n_experts
    sidx = jnp.arange(n_seg, dtype=jnp.int32)
    e_of = jnp.minimum(jnp.sum(send[None, :] <= sidx[:, None], axis=1), n_experts - 1).astype(jnp.int32)
    active = sidx < send[-1]
    local = sidx - sstart[e_of]
    nch = jnp.where(active, jnp.clip((pcounts[e_of] - local * C + rc - 1) // rc, 0, nch_max), 0)
    used = pstart[-1] + pcounts[-1]
    idle_row0 = used + (sidx - send[-1]) * C
    row0 = jnp.where(active, pstart[e_of] + local * C, jnp.minimum(idle_row0, R))
    nzero = jnp.where(active, 0, jnp.clip((R - idle_row0) // rc, 0, nch_max))
    seg_e = jnp.where(active, e_of, e_of[jnp.maximum(send[-1] - 1, 0)])
    i32 = jnp.int32
    return tok, dest, seg_e.astype(i32), row0.astype(i32), nch.astype(i32), nzero.astype(i32), R


def _moe_kernel(seg_e_ref, row0_ref, nch_ref, nzero_ref, tok_ref, x_hbm, wg_ref, wu_ref, wd_ref, y_hbm,
                xp, a_acc, u_acc, hbuf, wgb, wub, wdb, ostage, gsem, osem, *, rc, nk, nn, n_seg):
    del seg_e_ref
    s = pl.program_id(0)
    j = pl.program_id(1)
    nch = nch_ref[s]
    row0 = row0_ref[s]
    nzero = nzero_ref[s]
    cur = s % 2
    kt = wgb.shape[0]

    def chunk_rows(ch):
        return pl.ds(pl.multiple_of(ch * rc, rc), rc)

    def out_copy(slot, ch, n):
        dst = y_hbm.at[pl.ds(pl.multiple_of(row0 + ch * rc, rc), rc), n, :]
        return pltpu.make_async_copy(ostage.at[slot], dst, osem.at[slot])

    @pl.when((j == 0) & (nzero > 0))
    def _():
        ostage[0] = jnp.zeros(ostage.shape[1:], jnp.float32)

        def start(ch, carry):
            for n in range(nn):
                out_copy(0, ch, n).start()
            return carry

        def wait(ch, carry):
            for n in range(nn):
                out_copy(0, ch, n).wait()
            return carry
        lax.fori_loop(0, nzero, start, 0)
        lax.fori_loop(0, nzero, wait, 0)

    def gather(seg, buf, wait):
        base = row0_ref[seg]

        def body(r8, carry):
            for u in range(8):
                r = r8 * 8 + u
                cp = pltpu.make_async_copy(x_hbm.at[pl.ds(tok_ref[base + r], 1)], xp.at[buf, pl.ds(r, 1)],
                                           gsem.at[buf])
                if wait:
                    cp.wait()
                else:
                    cp.start()
            return carry
        lax.fori_loop(0, nch_ref[seg] * (rc // 8), body, 0)

    @pl.when((j == 0) & (nch > 0))
    def _():
        @pl.when(s == 0)
        def _():
            gather(0, 0, False)

        gather(s, cur, True)
        nxt = jnp.minimum(s + 1, n_seg - 1)

        @pl.when((s + 1 < n_seg) & (nch_ref[nxt] > 0))
        def _():
            gather(nxt, 1 - cur, False)

    @pl.when((j < nk) & (nch > 0))
    def _():
        wgb[...] = wg_ref[...].astype(wgb.dtype)
        wub[...] = wu_ref[...].astype(wub.dtype)

    for jj in range(nk):
        @pl.when((j == jj) & (nch > 0))
        def _(jj=jj):
            def body(ch, carry):
                rows = chunk_rows(ch)
                w = xp[cur, rows, (jj // 2) * kt:(jj // 2 + 1) * kt]
                if jj % 2 == 0:
                    xf = lax.bitcast_convert_type(lax.shift_left(w, jnp.uint32(16)), jnp.float32)
                else:
                    xf = lax.bitcast_convert_type(w & jnp.uint32(0xFFFF0000), jnp.float32)
                x = xf.astype(wgb.dtype)
                a = jnp.dot(x, wgb[...], preferred_element_type=jnp.float32)
                u = jnp.dot(x, wub[...], preferred_element_type=jnp.float32)
                if jj == 0:
                    a_acc[rows, :] = a
                    u_acc[rows, :] = u
                else:
                    a_acc[rows, :] += a
                    u_acc[rows, :] += u
                return carry
            lax.fori_loop(0, nch, body, 0)

    @pl.when((j == nk - 1) & (nch > 0))
    def _():
        def body(ch, carry):
            rows = chunk_rows(ch)
            a = a_acc[rows, :]
            hbuf[rows, :] = (a * jax.nn.sigmoid(a) * u_acc[rows, :]).astype(hbuf.dtype)
            return carry
        lax.fori_loop(0, nch, body, 0)

    @pl.when((j >= nk) & (nch > 0))
    def _():
        wdb[...] = wd_ref[...].astype(wdb.dtype)

    for n in range(nn):
        @pl.when((j == nk + n) & (nch > 0))
        def _(n=n):
            def body(ch, carry):
                slot = ch % 2

                @pl.when(ch >= 2)
                def _():
                    out_copy(slot, ch - 2, n).wait()

                ostage[slot] = jnp.dot(hbuf[chunk_rows(ch), :], wdb[...], preferred_element_type=jnp.float32)
                out_copy(slot, ch, n).start()
                return carry
            lax.fori_loop(0, nch, body, 0)

            @pl.when(nch >= 2)
            def _():
                out_copy(nch % 2, nch - 2, n).wait()

            out_copy((nch - 1) % 2, nch - 1, n).wait()


def _moe_experts(xpk, kt, plan, w_gate, w_up, w_down, layer):
    tok, _, seg_e, row0, nch, nzero, R = plan
    D = 2 * xpk.shape[1]
    nk = D // kt
    F = w_gate.shape[-1]
    nt = min(MOE_NT, D)
    nn = D // nt
    rc = MOE_ROWS
    C = rc * MOE_CHUNKS
    n_seg = seg_e.shape[0]

    def k_idx(s, j, nc):
        return jnp.where(nc[s] > 0, jnp.minimum(j, nk - 1), nk - 1)

    def n_idx(s, j, nc):
        return jnp.where(nc[s] > 0, jnp.maximum(j - nk, 0), nn - 1)

    grid_spec = pltpu.PrefetchScalarGridSpec(
        num_scalar_prefetch=5, grid=(n_seg, nk + nn),
        in_specs=[
            pl.BlockSpec(memory_space=pl.ANY),
            pl.BlockSpec((None, None, kt, F), lambda s, j, se, r0, nc, nz, tk: (layer, se[s], k_idx(s, j, nc), 0)),
            pl.BlockSpec((None, None, kt, F), lambda s, j, se, r0, nc, nz, tk: (layer, se[s], k_idx(s, j, nc), 0)),
            pl.BlockSpec((None, None, F, nt), lambda s, j, se, r0, nc, nz, tk: (layer, se[s], 0, n_idx(s, j, nc))),
        ],
        out_specs=pl.BlockSpec(memory_space=pl.ANY),
        scratch_shapes=[
            pltpu.VMEM((2, C, D // 2), jnp.uint32),
            pltpu.VMEM((C, F), jnp.float32),
            pltpu.VMEM((C, F), jnp.float32),
            pltpu.VMEM((C, F), MXU_DTYPE),
            pltpu.VMEM((kt, F), MXU_DTYPE),
            pltpu.VMEM((kt, F), MXU_DTYPE),
            pltpu.VMEM((F, nt), MXU_DTYPE),
            pltpu.VMEM((2, rc, nt), jnp.float32),
            pltpu.SemaphoreType.DMA((2,)),
            pltpu.SemaphoreType.DMA((2,)),
        ])
    return pl.pallas_call(
        functools.partial(_moe_kernel, rc=rc, nk=nk, nn=nn, n_seg=n_seg), grid_spec=grid_spec,
        out_shape=jax.ShapeDtypeStruct((R, nn, nt), jnp.float32),
        compiler_params=_cparams("arbitrary", "arbitrary"), name="moe_experts",
    )(seg_e, row0, nch, nzero, tok, xpk, w_gate, w_up, w_down)


def _combine_kernel(dest_ref, x_ref, gate_ref, gam_ref, bet_ref, y_hbm, o32_ref, o16_ref, ybuf, sem,
                    *, tc, alpha, n_tiles):
    i = pl.program_id(0)
    cur = i % 2

    def rows(tile, buf, wait):
        def body(r, carry):
            for k in range(EXPERT_TOPK):
                slot = dest_ref[(tile * tc + r) * EXPERT_TOPK + k]
                cp = pltpu.make_async_copy(y_hbm.at[slot], ybuf.at[buf, k, r], sem.at[buf])
                if wait:
                    cp.wait()
                else:
                    cp.start()
            return carry
        lax.fori_loop(0, tc, body, 0)

    @pl.when(i == 0)
    def _():
        rows(0, 0, False)

    rows(i, cur, True)

    @pl.when(i + 1 < n_tiles)
    def _():
        rows(i + 1, 1 - cur, False)

    gate = gate_ref[...]
    nn, nt = ybuf.shape[3], ybuf.shape[4]
    parts = []
    for n in range(nn):
        ffn = gate[:, 0:1] * ybuf[cur, 0, :, n, :]
        for k in range(1, EXPERT_TOPK):
            ffn = ffn + gate[:, k:k + 1] * ybuf[cur, k, :, n, :]
        parts.append(alpha * x_ref[:, n * nt:(n + 1) * nt] + ffn)
    v = jnp.concatenate(parts, axis=1)
    mu = jnp.mean(v, axis=-1, keepdims=True)
    d = v - mu
    var = jnp.mean(d * d, axis=-1, keepdims=True)
    y = d * lax.rsqrt(var + LN_EPS) * gam_ref[...] + bet_ref[...]
    o32_ref[...] = y
    o16_ref[...] = y.astype(o16_ref.dtype)


def _moe_combine(dest, x32, gates, y_sorted, gam, bet, alpha):
    Np, D = x32.shape
    tc = _pick_tile(Np, 192, 16)
    row = lambda i, d: (i, 0)
    grid_spec = pltpu.PrefetchScalarGridSpec(
        num_scalar_prefetch=1, grid=(Np // tc,),
        in_specs=[
            pl.BlockSpec((tc, D), row),
            pl.BlockSpec((tc, LANE), row),
            pl.BlockSpec((1, D), lambda i, d: (0, 0)),
            pl.BlockSpec((1, D), lambda i, d: (0, 0)),
            pl.BlockSpec(memory_space=pl.ANY),
        ],
        out_specs=[pl.BlockSpec((tc, D), row), pl.BlockSpec((tc, D), row)],
        scratch_shapes=[pltpu.VMEM((2, EXPERT_TOPK, tc) + y_sorted.shape[1:], jnp.float32),
                        pltpu.SemaphoreType.DMA((2,))])
    return pl.pallas_call(
        functools.partial(_combine_kernel, tc=tc, alpha=alpha, n_tiles=Np // tc), grid_spec=grid_spec,
        out_shape=[jax.ShapeDtypeStruct((Np, D), jnp.float32), jax.ShapeDtypeStruct((Np, D), MXU_DTYPE)],
        compiler_params=_cparams("arbitrary"), name="moe_combine",
    )(dest, x32, gates, gam.reshape(1, D), bet.reshape(1, D), y_sorted)


def _rope_tables(pos, scale_wi):
    def cs(dim):
        half = dim // 2
        inv = ROPE_THETA ** (-jnp.arange(half, dtype=jnp.float32) * 2.0 / dim)
        ang = pos.astype(jnp.float32)[:, None] * inv[None, :]
        return jnp.cos(ang), jnp.sin(ang)
    c, s = cs(HEAD_DIM)
    t128 = (jnp.concatenate([c, c], -1), jnp.concatenate([-s, s], -1))
    c, s = cs(IDX_DIM)
    c64 = jnp.concatenate([c, c], -1)
    s64 = jnp.concatenate([-s, s], -1)
    t64 = (jnp.tile(c64, (1, LANE // IDX_DIM)), jnp.tile(s64, (1, LANE // IDX_DIM)))
    n = pos.shape[0]
    pad = LANE - IDX_DIM - IDX_HEADS
    tkw = (jnp.concatenate([c64, jnp.full((n, IDX_HEADS), scale_wi, jnp.float32), jnp.zeros((n, pad), jnp.float32)], -1),
           jnp.concatenate([s64, jnp.zeros((n, LANE - IDX_DIM), jnp.float32)], -1))
    return t128, t64, tkw


def kernel(x_prompt, x_sample, cache_a_k, cache_a_v, cache_a_kidx, cache_b_k, cache_b_v, page_table, p_prompt, p_sample, ln_emb_g, ln_emb_b, w_in, w_out, ln1_g, ln1_b, w_route_group, w_route_expert, w_exp_gate, w_exp_up, w_exp_down, ln2_g, ln2_b, w_ple, w_ple_gate, ln3_g, ln3_b):
    B, T, D = x_prompt.shape
    Bd, Td, _ = x_sample.shape
    depth = w_in.shape[0]
    n_phys, page = cache_a_k.shape[1], cache_a_k.shape[2]
    past = page_table.shape[1] * page
    n_experts = w_exp_gate.shape[1]
    h_a = D // 2 // HEAD_DIM
    h_b = h_a
    rep_a, rep_b = h_a // KV_A, h_b // KV_B
    assert IDX_DIM * 2 == LANE and IDX_DIM + IDX_HEADS <= LANE and page == LANE
    assert N_GROUPS * (1 + EXPERTS_PER_GROUP) <= LANE and n_experts == N_GROUPS * EXPERTS_PER_GROUP
    alpha = (2.0 * depth) ** 0.25
    n_p, n_s = B * T, Bd * Td
    N = n_p + n_s
    Np = _round_up(N, ROW_ALIGN)
    f32 = jnp.float32

    def stream(a_p, a_s):
        w = a_p.shape[-1]
        return jnp.concatenate([a_p.reshape(n_p, w), a_s.reshape(n_s, w), jnp.zeros((Np - N, w), a_p.dtype)], 0)

    pos = jnp.concatenate([jnp.tile(jnp.arange(T, dtype=jnp.int32), B),
                           jnp.tile(past + jnp.arange(Td, dtype=jnp.int32), Bd),
                           jnp.zeros((Np - N,), jnp.int32)])
    t128, t64, tkw = _rope_tables(pos, IDX_HEADS ** -0.5 * IDX_DIM ** -0.5)

    widths = (h_a * HEAD_DIM, KV_A * HEAD_DIM, KV_A * HEAD_DIM, IDX_HEADS * IDX_DIM, IDX_DIM, IDX_HEADS,
              h_b * HEAD_DIM, KV_B * HEAD_DIM, KV_B * HEAD_DIM)
    offs = [0]
    for w in widths:
        offs.append(offs[-1] + w)
    assert offs[-1] == w_in.shape[2]

    def cols(w, *ids):
        return jnp.concatenate([w[:, offs[i]:offs[i + 1]] for i in ids], axis=1).astype(MXU_DTYPE)

    kv_w = KV_A * HEAD_DIM
    trow = jnp.arange(Td, dtype=jnp.int32)[:, None]
    own_bias = jnp.where(jnp.arange(LANE, dtype=jnp.int32)[None, :] <= trow, 0.0, NEG_BIAS).astype(f32)[None]

    def fresh(rows):
        w = rows.shape[-1]
        return jnp.pad(rows.reshape(Bd, Td, w), ((0, 0), (0, LANE - Td), (0, 0)))

    def decode_q(q_sm, kv, rep):
        q = q_sm.reshape(kv, rep, Bd, Td, HEAD_DIM)
        return jnp.transpose(q, (2, 0, 1, 3, 4)).reshape(Bd, kv, rep * Td, HEAD_DIM)

    def decode_o(o, kv, rep):
        o = o.reshape(Bd, kv, rep, Td, HEAD_DIM)
        return jnp.transpose(o, (0, 3, 1, 2, 4)).reshape(n_s, kv * rep * HEAD_DIM)

    x_tail = jnp.concatenate([x_sample.reshape(n_s, D), jnp.zeros((Np - N, D), x_sample.dtype)], axis=0)
    h32, h16 = _layer_norm_embed(x_prompt.reshape(n_p, D), x_tail, ln_emb_g, ln_emb_b)
    rows_out = []
    for l in range(depth):
        wl = w_in[l]
        q128 = _matmul([h16], [cols(wl, 0, 6)], MXU_DTYPE, "rope128", t128, True, name="proj_q")
        k128 = _matmul([h16], [cols(wl, 1, 7)], f32, "rope128", t128, name="proj_k")
        v128 = _matmul([h16], [cols(wl, 2, 8)], f32, name="proj_v")
        qi = _matmul([h16], [cols(wl, 3)], MXU_DTYPE, "rope64", t64, True, name="proj_qi")
        w_kw = jnp.pad(cols(wl, 4, 5), ((0, 0), (0, LANE - IDX_DIM - IDX_HEADS)))
        kiwi = _matmul([h16], [w_kw], f32, "rope64", tkw, name="proj_kiwi")

        attn_a = _dsa_prompt(q128, k128, v128, qi, kiwi, B, T, rep_a)
        attn_b = _moba_prompt(q128, k128, v128, B, T, rep_b, h_a)

        ks, vs = k128[n_p:N], v128[n_p:N]
        kiwi_s = kiwi[n_p:N]
        qi_s = qi[:, n_p:N].reshape(IDX_HEADS // 2, Bd, Td, 2, IDX_DIM)
        qi_s = jnp.transpose(qi_s, (1, 0, 3, 2, 4)).reshape(Bd, IDX_HEADS * Td, IDX_DIM)
        wcol = jnp.transpose(kiwi_s[:, IDX_DIM:IDX_DIM + IDX_HEADS].reshape(Bd, Td, IDX_HEADS), (0, 2, 1))
        wcol = wcol.reshape(Bd, IDX_HEADS * Td, 1)
        bias_p, bias_n = _dsa_decode_mask(page_table, qi_s, wcol, cache_a_kidx, fresh(kiwi_s[:, :IDX_DIM]), l, Td)
        o_a = _paged_attention(page_table, decode_q(q128[:h_a, n_p:N], KV_A, rep_a), cache_a_k, cache_a_v,
                               fresh(ks[:, :kv_w]), fresh(vs[:, :kv_w]), bias_p[:, :, None], bias_n, l, Td, rep_a)
        qb_s = decode_q(q128[h_a:, n_p:N], KV_B, rep_b)
        bias_b = _moba_decode_mask(page_table, qb_s, cache_b_k, l, Td, rep_b)
        o_b = _paged_attention(page_table, qb_s, cache_b_k, cache_b_v, fresh(ks[:, kv_w:]), fresh(vs[:, kv_w:]),
                               bias_b, own_bias, l, Td, rep_b)
        tail = jnp.zeros((Np - N, h_a * HEAD_DIM), MXU_DTYPE)
        attn_a = jnp.concatenate([attn_a, decode_o(o_a, KV_A, rep_a).astype(MXU_DTYPE), tail], axis=0)
        attn_b = jnp.concatenate([attn_b, decode_o(o_b, KV_B, rep_b).astype(MXU_DTYPE), tail], axis=0)

        w_o = w_out[l].astype(MXU_DTYPE)
        mix = _matmul([attn_a, attn_b], [w_o[:h_a * HEAD_DIM], w_o[h_a * HEAD_DIM:]], f32, name="proj_out")
        w_r = jnp.pad(jnp.concatenate([w_route_group[l], w_route_expert[l]], axis=1),
                      ((0, 0), (0, LANE - N_GROUPS - n_experts)))
        w_rh = w_r.astype(MXU_DTYPE)
        w_rl = (w_r - w_rh.astype(f32)).astype(MXU_DTYPE)
        moe_kt = min(MOE_KT, D // 2)
        x32, xpk, logits = _layer_norm([h32, mix], ln1_g[l], ln1_b[l], "add", alpha, (w_rh, w_rl), pack_kt=moe_kt)

        eid, gates = _route(logits)
        plan = _moe_plan(eid[:N, :EXPERT_TOPK].reshape(-1), n_experts, MOE_ROWS, MOE_CHUNKS)
        y_sorted = _moe_experts(xpk, moe_kt, plan, w_exp_gate, w_exp_up, w_exp_down, l)
        dest = jnp.pad(plan[1], (0, (Np - N) * EXPERT_TOPK))
        x32, x16 = _moe_combine(dest, x32, gates, y_sorted, ln2_g[l], ln2_b[l], alpha)

        gate_pre = _matmul([x16], [w_ple_gate[l].astype(MXU_DTYPE)], f32, name="ple_gate")
        p16 = stream(p_prompt[l], p_sample[l]).astype(MXU_DTYPE)
        ple = _matmul([p16], [w_ple[l].astype(MXU_DTYPE)], f32, name="ple_embed")
        h32, h16 = _layer_norm([x32, gate_pre, ple], ln3_g[l], ln3_b[l], "ple", alpha)
        rows_out.append((k128, v128, kiwi))

    def gather_rows(sel, lo, hi, lead):
        return jnp.stack([sel(r)[lo:hi].reshape(lead) for r in rows_out])

    outs = [h32[:n_p].reshape(B, T, D), h32[n_p:N].reshape(Bd, Td, D)]
    for lo, hi, lead in ((0, n_p, (B, T)), (n_p, N, (Bd, Td))):
        outs += [
            gather_rows(lambda r: r[0][:, :kv_w], lo, hi, lead + (KV_A, HEAD_DIM)),
            gather_rows(lambda r: r[1][:, :kv_w], lo, hi, lead + (KV_A, HEAD_DIM)),
            gather_rows(lambda r: r[2][:, :IDX_DIM], lo, hi, lead + (IDX_DIM,)),
            gather_rows(lambda r: r[0][:, kv_w:], lo, hi, lead + (KV_B, HEAD_DIM)),
            gather_rows(lambda r: r[1][:, kv_w:], lo, hi, lead + (KV_B, HEAD_DIM)),
        ]
    return tuple(outs)
```

```python
import functools

import jax
import jax.numpy as jnp
from jax import lax
from jax.experimental import pallas as pl
from jax.experimental.pallas import tpu as pltpu

HEAD_DIM = 128
KV_A = 4
IDX_HEADS = 32
IDX_DIM = 64
DSA_TOPK = 256
KV_B = 4
MOBA_BLOCK = 256
MOBA_TOPK = 3
N_GROUPS = 4
EXPERTS_PER_GROUP = 8
EXPERT_TOPK = 2
ROPE_THETA = 10000.0
LN_EPS = 1e-5

LANE = 128
ROW_ALIGN = 256
MXU_DTYPE = jnp.bfloat16
NEG_BIAS = -1e30
INT_MIN = -2147483648
VMEM_LIMIT = 56 * 1024 * 1024
PAGES_PER_CHUNK = 16
MOE_ROWS = 128
MOE_CHUNKS = 5
GATHER_DMA_PRIORITY = 1
MOE_KT = 1024
MOE_NT = 1024


def _cparams(*sem):
    return pltpu.CompilerParams(dimension_semantics=sem, vmem_limit_bytes=VMEM_LIMIT)


def _round_up(n, m):
    return (n + m - 1) // m * m


def _pick_tile(n, cap, mult):
    best = None
    for t in range(mult, cap + 1, mult):
        if n % t == 0:
            best = t
    assert best is not None, (n, cap, mult)
    return best


def _dot_nt(a, b):
    return lax.dot_general(a, b, (((1,), (1,)), ((), ())), preferred_element_type=jnp.float32)


def _ln_kernel(*refs, mode, alpha, router, pack_kt):
    it = iter(refs)
    x_ref = next(it)
    a_ref = next(it) if mode == "add" else None
    g_ref = next(it) if mode == "ple" else None
    p_ref = next(it) if mode == "ple" else None
    gam_ref, bet_ref = next(it), next(it)
    wh_ref = next(it) if router else None
    wl_ref = next(it) if router else None
    o32_ref, o16_ref = next(it), next(it)
    lg_ref = next(it) if router else None

    v = x_ref[...]
    if mode == "add":
        v = alpha * v + a_ref[...]
    elif mode == "ple":
        v = alpha * v + jax.nn.sigmoid(g_ref[...]) * p_ref[...]
    mu = jnp.mean(v, axis=-1, keepdims=True)
    d = v - mu
    var = jnp.mean(d * d, axis=-1, keepdims=True)
    y = d * lax.rsqrt(var + LN_EPS) * gam_ref[...] + bet_ref[...]
    o32_ref[...] = y
    if pack_kt is not None:
        kt = pack_kt
        for b in range(o16_ref.shape[1] // kt):
            lo = y[:, 2 * b * kt:(2 * b + 1) * kt].astype(jnp.bfloat16).astype(jnp.float32)
            hi = y[:, (2 * b + 1) * kt:(2 * b + 2) * kt].astype(jnp.bfloat16).astype(jnp.float32)
            o16_ref[:, b * kt:(b + 1) * kt] = (
                lax.shift_right_logical(lax.bitcast_convert_type(lo, jnp.uint32), jnp.uint32(16))
                | (lax.bitcast_convert_type(hi, jnp.uint32) & jnp.uint32(0xFFFF0000)))
    else:
        o16_ref[...] = y.astype(o16_ref.dtype)
    if router:
        yh = y.astype(MXU_DTYPE)
        yl = (y - yh.astype(jnp.float32)).astype(MXU_DTYPE)
        wh = wh_ref[...]
        lg = jnp.dot(yh, wh, preferred_element_type=jnp.float32)
        lg = lg + jnp.dot(yl, wh, preferred_element_type=jnp.float32)
        lg = lg + jnp.dot(yh, wl_ref[...], preferred_element_type=jnp.float32)
        lg_ref[...] = lg


def _ln_embed_kernel(xh_ref, xt_ref, gam_ref, bet_ref, o32_ref, o16_ref, *, n_head):
    v = jnp.where(pl.program_id(0) < n_head, xh_ref[...], xt_ref[...])
    mu = jnp.mean(v, axis=-1, keepdims=True)
    d = v - mu
    var = jnp.mean(d * d, axis=-1, keepdims=True)
    y = d * lax.rsqrt(var + LN_EPS) * gam_ref[...] + bet_ref[...]
    o32_ref[...] = y
    o16_ref[...] = y.astype(o16_ref.dtype)


def _layer_norm_embed(x_head, x_tail, gam, bet):
    n_h, D = x_head.shape
    n_t = x_tail.shape[0]
    tr = _pick_tile(ROW_ALIGN, 128, 16)
    assert n_h % tr == 0 and n_t % tr == 0
    nh, ntl = n_h // tr, n_t // tr
    vec = pl.BlockSpec((1, D), lambda i: (0, 0))
    row = pl.BlockSpec((tr, D), lambda i: (i, 0))
    return pl.pallas_call(
        functools.partial(_ln_embed_kernel, n_head=nh), grid=(nh + ntl,),
        in_specs=[pl.BlockSpec((tr, D), lambda i: (jnp.minimum(i, nh - 1), 0)),
                  pl.BlockSpec((tr, D), lambda i: (jnp.maximum(i - nh, 0), 0)), vec, vec],
        out_specs=[row, row],
        out_shape=[jax.ShapeDtypeStruct((n_h + n_t, D), jnp.float32),
                   jax.ShapeDtypeStruct((n_h + n_t, D), MXU_DTYPE)],
        compiler_params=_cparams("parallel"), name="ln_embed")(x_head, x_tail, gam.reshape(1, D), bet.reshape(1, D))


def _layer_norm(xs, gam, bet, mode, alpha=1.0, router_w=None, pack_kt=None):
    Np, D = xs[0].shape
    tr = _pick_tile(Np, 192, 16)
    row = pl.BlockSpec((tr, D), lambda i: (i, 0))
    vec = pl.BlockSpec((1, D), lambda i: (0, 0))
    in_specs = [row] * len(xs) + [vec, vec]
    args = list(xs) + [gam.reshape(1, D), bet.reshape(1, D)]
    if pack_kt is None:
        out_shape = [jax.ShapeDtypeStruct((Np, D), jnp.float32), jax.ShapeDtypeStruct((Np, D), MXU_DTYPE)]
        out_specs = [row, row]
    else:
        assert D % (2 * pack_kt) == 0
        out_shape = [jax.ShapeDtypeStruct((Np, D), jnp.float32), jax.ShapeDtypeStruct((Np, D // 2), jnp.uint32)]
        out_specs = [row, pl.BlockSpec((tr, D // 2), lambda i: (i, 0))]
    if router_w is not None:
        wspec = pl.BlockSpec((D, LANE), lambda i: (0, 0))
        in_specs += [wspec, wspec]
        args += list(router_w)
        out_shape.append(jax.ShapeDtypeStruct((Np, LANE), jnp.float32))
        out_specs.append(pl.BlockSpec((tr, LANE), lambda i: (i, 0)))
    return pl.pallas_call(
        functools.partial(_ln_kernel, mode=mode, alpha=alpha, router=router_w is not None, pack_kt=pack_kt),
        grid=(Np // tr,), in_specs=in_specs, out_specs=out_specs, out_shape=out_shape,
        compiler_params=_cparams("parallel"), name="ln_" + mode)(*args)


def _mm_kernel(*refs, n_x, mode, slice_major):
    x_refs = refs[:n_x]
    w_refs = refs[n_x:2 * n_x]
    rest = refs[2 * n_x:]
    if mode == "none":
        (o_ref,) = rest
    else:
        cos_ref, sin_ref, o_ref = rest
    y = jnp.dot(x_refs[0][...], w_refs[0][...], preferred_element_type=jnp.float32)
    for k in range(1, n_x):
        y = y + jnp.dot(x_refs[k][...], w_refs[k][...], preferred_element_type=jnp.float32)
    if mode == "none" and not slice_major:
        o_ref[...] = y.astype(o_ref.dtype)
        return
    tm = y.shape[0]
    if mode != "none":
        cos = cos_ref[...]
        sin = sin_ref[...]
    if mode == "rope64":
        lane = lax.broadcasted_iota(jnp.int32, (tm, LANE), 1)
        first_half = (lane % 64) < 32
    for s in range(y.shape[1] // LANE):
        yh = y[:, s * LANE:(s + 1) * LANE]
        if mode == "rope128":
            yh = yh * cos + pltpu.roll(yh, 64, 1) * sin
        elif mode == "rope64":
            partner = jnp.where(first_half, pltpu.roll(yh, 96, 1), pltpu.roll(yh, 32, 1))
            yh = yh * cos + partner * sin
        if slice_major:
            o_ref[s] = yh.astype(o_ref.dtype)
        else:
            o_ref[:, s * LANE:(s + 1) * LANE] = yh.astype(o_ref.dtype)


def _matmul(xs, ws, out_dtype, mode="none", tables=None, slice_major=False, name="mm"):
    Np = xs[0].shape[0]
    Nc = ws[0].shape[1]
    tm = _pick_tile(Np, 640, 16)
    tn = _pick_tile(Nc, 1024, LANE)
    in_specs = [pl.BlockSpec((tm, x.shape[1]), lambda j, i: (i, 0)) for x in xs]
    in_specs += [pl.BlockSpec((w.shape[0], tn), lambda j, i: (0, j)) for w in ws]
    args = list(xs) + list(ws)
    if mode != "none":
        in_specs += [pl.BlockSpec((tm, LANE), lambda j, i: (i, 0))] * 2
        args += list(tables)
    if slice_major:
        out_shape = jax.ShapeDtypeStruct((Nc // LANE, Np, LANE), out_dtype)
        out_spec = pl.BlockSpec((tn // LANE, tm, LANE), lambda j, i: (j, i, 0))
    else:
        out_shape = jax.ShapeDtypeStruct((Np, Nc), out_dtype)
        out_spec = pl.BlockSpec((tm, tn), lambda j, i: (i, j))
    return pl.pallas_call(
        functools.partial(_mm_kernel, n_x=len(xs), mode=mode, slice_major=slice_major),
        grid=(Nc // tn, Np // tm), in_specs=in_specs, out_specs=out_spec, out_shape=out_shape,
        compiler_params=_cparams("parallel", "parallel"), name=name)(*args)


def _float_keys(x):
    b = lax.bitcast_convert_type(x, jnp.int32)
    return jnp.where(b < 0, b ^ jnp.int32(0x7FFFFFFF), b)


def _kth_largest(count_ge, shape, k):
    def body(it, ans):
        cand = ans + lax.shift_left(jnp.int32(1), 31 - it)
        return jnp.where(count_ge(cand) >= k, cand, ans)
    return lax.fori_loop(0, 32, body, jnp.full(shape, INT_MIN, jnp.int32))


def _topk_lanes(gate, k):
    lane = lax.broadcasted_iota(jnp.int32, gate.shape, 1)
    sel = jnp.zeros(gate.shape, jnp.float32)
    g = gate
    for _ in range(k):
        m = jnp.max(g, axis=-1, keepdims=True)
        idx = jnp.min(jnp.where(g == m, lane, LANE), axis=-1, keepdims=True)
        hit = lane == idx
        sel = jnp.where(hit, 1.0, sel)
        g = jnp.where(hit, -jnp.inf, g)
    return sel


def _attend(q_ref, k_ref, v_ref, bias_ref, o_ref, rep, scale, width):
    k = k_ref[:width, :].astype(MXU_DTYPE)
    v = v_ref[:width, :].astype(MXU_DTYPE)
    bias = bias_ref[:, :width]
    for r in range(rep):
        s = _dot_nt(q_ref[r], k) * scale + bias
        m = jnp.max(s, axis=-1, keepdims=True)
        p = jnp.exp(s - m)
        l = jnp.sum(p, axis=-1, keepdims=True)
        o = jnp.dot(p.astype(MXU_DTYPE), v, preferred_element_type=jnp.float32) / l
        o_ref[:, r * HEAD_DIM:(r + 1) * HEAD_DIM] = o.astype(o_ref.dtype)


def _causal_extents(n_tiles, max_branches=4):
    nbr = min(max_branches, n_tiles)
    out, lo = [], 0
    for hi in sorted({-(-n_tiles * (b + 1) // nbr) for b in range(nbr)}):
        out.append((lo, hi))
        lo = hi
    return out


def _dsa_select(qi_ref, wq_ref, kiw_ref, bias_ref, key_ref, i, tq, W, n_keep):
    klane = lax.broadcasted_iota(jnp.int32, (W, LANE), 1)
    ka32 = jnp.where(klane < IDX_DIM, kiw_ref[:W, :], 0.0)
    ka = ka32.astype(MXU_DTYPE)
    kb = pltpu.roll(ka32, IDX_DIM, 1).astype(MXU_DTYPE)
    wq = wq_ref[...]
    wlane = lax.broadcasted_iota(jnp.int32, (tq, LANE), 1)

    def head_weight(h):
        return jnp.sum(jnp.where(wlane == IDX_DIM + h, wq, 0.0), axis=-1, keepdims=True)

    def pair(hp, carry):
        qp = qi_ref[hp]
        c = (jnp.maximum(_dot_nt(qp, ka), 0.0) * head_weight(2 * hp)
             + jnp.maximum(_dot_nt(qp, kb), 0.0) * head_weight(2 * hp + 1))

        bias_ref[:, :W] += c
        return carry
    bias_ref[:, :W] = jnp.zeros((tq, W), jnp.float32)
    lax.fori_loop(0, IDX_HEADS // 2, pair, 0)
    score = bias_ref[:, :W]
    row = lax.broadcasted_iota(jnp.int32, (tq, W), 0) + i * tq
    col = lax.broadcasted_iota(jnp.int32, (tq, W), 1)
    causal = col <= row
    key_ref[:, :W] = jnp.where(causal, _float_keys(score), INT_MIN)

    def count_ge(cand):
        return jnp.sum(jnp.where(key_ref[:, :W] >= cand, 1, 0), axis=-1, keepdims=True)

    thr = _kth_largest(count_ge, (tq, 1), n_keep)
    keys = key_ref[:, :W]
    gt = keys > thr
    eq = keys == thr
    n_gt = jnp.sum(jnp.where(gt, 1, 0), axis=-1, keepdims=True)
    n_eq = jnp.sum(jnp.where(eq, 1, 0), axis=-1, keepdims=True)
    bias_ref[:, :W] = jnp.where((gt | eq) & causal, 0.0, NEG_BIAS)
    tie = (thr > INT_MIN) & (n_gt + n_eq > n_keep)

    @pl.when(jnp.max(jnp.where(tie, 1, 0)) > 0)
    def _():
        need = n_keep - n_gt
        nbits = W.bit_length()

        def body(it, lim):
            cand = lim + lax.shift_left(jnp.int32(1), nbits - 1 - it)
            c = jnp.sum(jnp.where(eq & (col < cand), 1, 0), axis=-1, keepdims=True)
            return jnp.where(c <= need, cand, lim)

        lim = lax.fori_loop(0, nbits, body, jnp.zeros((tq, 1), jnp.int32))
        bias_ref[:, :W] = jnp.where((gt | (eq & (col < lim))) & causal, 0.0, NEG_BIAS)


def _dsa_prompt_kernel(q_ref, k_ref, v_ref, qi_ref, wq_ref, kiw_ref, o_ref, bias_ref, key_ref,
                       *, tq, T, rep, n_keep, scale):
    i = pl.program_id(1)
    g = pl.program_id(2)
    for lo, hi in _causal_extents(T // tq):
        @pl.when((i >= lo) & (i < hi))
        def _(W=hi * tq):
            @pl.when(g == 0)
            def _():
                _dsa_select(qi_ref, wq_ref, kiw_ref, bias_ref, key_ref, i, tq, W, n_keep)

            _attend(q_ref, k_ref, v_ref, bias_ref, o_ref, rep, scale, W)


def _dsa_prompt(q128, k128, v128, qi, kiwi, B, T, rep):
    tq = min(256, T)
    assert T % tq == 0 and tq % LANE == 0
    nT = T // tq
    n_keep = min(DSA_TOPK, T // 4)
    kern = functools.partial(_dsa_prompt_kernel, tq=tq, T=T, rep=rep, n_keep=n_keep,
                             scale=HEAD_DIM ** -0.5)
    return pl.pallas_call(
        kern, grid=(B, nT, KV_A),
        in_specs=[
            pl.BlockSpec((rep, tq, HEAD_DIM), lambda b, i, g: (g, b * nT + i, 0)),
            pl.BlockSpec((T, HEAD_DIM), lambda b, i, g: (b, g)),
            pl.BlockSpec((T, HEAD_DIM), lambda b, i, g: (b, g)),
            pl.BlockSpec((IDX_HEADS // 2, tq, LANE), lambda b, i, g: (0, b * nT + i, 0)),
            pl.BlockSpec((tq, LANE), lambda b, i, g: (b * nT + i, 0)),
            pl.BlockSpec((T, LANE), lambda b, i, g: (b, 0)),
        ],
        out_specs=pl.BlockSpec((tq, rep * HEAD_DIM), lambda b, i, g: (b * nT + i, g)),
        out_shape=jax.ShapeDtypeStruct((B * T, KV_A * rep * HEAD_DIM), MXU_DTYPE),
        scratch_shapes=[pltpu.VMEM((tq, T), jnp.float32), pltpu.VMEM((tq, T), jnp.int32)],
        compiler_params=_cparams("parallel", "parallel", "arbitrary"), name="dsa_prompt",
    )(q128, k128, v128, qi, kiwi, kiwi)


def _moba_prompt_kernel(q_ref, k_ref, v_ref, o_ref, bias_ref, *, bs, nblk, rep, n_pick, scale):
    i = pl.program_id(1)
    qsum = q_ref[0].astype(jnp.float32)
    for r in range(1, rep):
        qsum = qsum + q_ref[r].astype(jnp.float32)
    lane = lax.broadcasted_iota(jnp.int32, (bs, LANE), 1)
    past = lane < i
    row = lax.broadcasted_iota(jnp.int32, (bs, bs), 0)
    col = lax.broadcasted_iota(jnp.int32, (bs, bs), 1)
    own_bias = jnp.where(col <= row, 0.0, NEG_BIAS)
    for lo, hi in _causal_extents(nblk):
        @pl.when((i >= lo) & (i < hi))
        def _(nb=hi):
            gate = jnp.full((bs, LANE), -jnp.inf, jnp.float32)
            for n in range(nb):
                mean_n = jnp.mean(k_ref[n * bs:(n + 1) * bs, :], axis=0, keepdims=True)
                gate = jnp.where(lane == n, jnp.sum(qsum * mean_n, axis=-1, keepdims=True), gate)
            sel = jnp.where(past, _topk_lanes(jnp.where(past, gate, -jnp.inf), n_pick), 0.0)
            for n in range(nb):
                picked = jnp.where(sel[:, n:n + 1] > 0.5, 0.0, NEG_BIAS)
                bias_ref[:, n * bs:(n + 1) * bs] = jnp.where(i == n, own_bias, jnp.broadcast_to(picked, (bs, bs)))
            _attend(q_ref, k_ref, v_ref, bias_ref, o_ref, rep, scale, nb * bs)


def _moba_prompt(q128, k128, v128, B, T, rep, h_a):
    bs = MOBA_BLOCK
    assert T % bs == 0
    nblk = T // bs
    assert 1 <= nblk <= LANE
    kern = functools.partial(_moba_prompt_kernel, bs=bs, nblk=nblk, rep=rep,
                             n_pick=min(MOBA_TOPK, nblk), scale=HEAD_DIM ** -0.5)
    return pl.pallas_call(
        kern, grid=(B, nblk, KV_B),
        in_specs=[
            pl.BlockSpec((rep, bs, HEAD_DIM), lambda b, i, g: (h_a // rep + g, b * nblk + i, 0)),
            pl.BlockSpec((T, HEAD_DIM), lambda b, i, g: (b, KV_A + g)),
            pl.BlockSpec((T, HEAD_DIM), lambda b, i, g: (b, KV_A + g)),
        ],
        out_specs=pl.BlockSpec((bs, rep * HEAD_DIM), lambda b, i, g: (b * nblk + i, g)),
        out_shape=jax.ShapeDtypeStruct((B * T, KV_B * rep * HEAD_DIM), MXU_DTYPE),
        scratch_shapes=[pltpu.VMEM((bs, T), jnp.float32)],
        compiler_params=_cparams("parallel", "parallel", "parallel"), name="moba_prompt",
    )(q128, k128, v128)


def _page_copies(pt_ref, pool_ref, layer, buf, sem, b, chunk, slot, pages, page, groups=0):
    out = []
    for p in range(pages):
        phys = pt_ref[b, chunk * pages + p]
        rows = pl.ds(p * page, page)
        if not groups:
            out.append(pltpu.make_async_copy(pool_ref.at[layer, phys], buf.at[slot, rows], sem.at[slot]))
        for g in range(groups):
            out.append(pltpu.make_async_copy(pool_ref.at[layer, phys, :, g, :], buf.at[slot, g, rows], sem.at[slot]))
    return out


def _paged_step(c, nchunk, start_fn, wait_fn):
    @pl.when(c == 0)
    def _():
        start_fn(0, 0)

    slot = c % 2
    wait_fn(c, slot)

    @pl.when(c + 1 < nchunk)
    def _():
        start_fn(c + 1, 1 - slot)

    return slot


def _dsa_decode_mask_kernel(pt_ref, qi_ref, wc_ref, pool_ref, knew_ref, bp_ref, bn_ref,
                            kbuf, sem, sp_ref, sn_ref, *, layer, nchunk, pages, page, Td, n_keep, past):
    b = pl.program_id(0)
    c = pl.program_id(1)
    CH = pages * page

    def start_fn(cc, slot):
        for cp in _page_copies(pt_ref, pool_ref, layer, kbuf, sem, b, cc, slot, pages, page):
            cp.start()

    def wait_fn(cc, slot):
        for cp in _page_copies(pt_ref, pool_ref, layer, kbuf, sem, b, cc, slot, pages, page):
            cp.wait()

    def scores(kc, width):
        d = _dot_nt(qi_ref[0], kc.astype(MXU_DTYPE))
        r = jnp.maximum(d, 0.0) * wc_ref[0]
        return jnp.sum(r.reshape(IDX_HEADS, Td, width), axis=0)

    @pl.when(c < nchunk)
    def _():
        slot = _paged_step(c, nchunk, start_fn, wait_fn)
        sp_ref[c] = scores(kbuf[slot], CH)

    @pl.when(c == nchunk)
    def _():
        sn_ref[...] = scores(knew_ref[0], LANE)
        trow = lax.broadcasted_iota(jnp.int32, (Td, LANE), 0)
        ncol = lax.broadcasted_iota(jnp.int32, (Td, LANE), 1)
        new_ok = ncol <= trow
        kp = _float_keys(sp_ref[...])
        kn = jnp.where(new_ok, _float_keys(sn_ref[...]), INT_MIN)

        def count(mp, mn):
            return (jnp.sum(jnp.sum(jnp.where(mp, 1, 0), axis=0), axis=-1, keepdims=True)
                    + jnp.sum(jnp.where(mn, 1, 0), axis=-1, keepdims=True))

        thr = _kth_largest(lambda cand: count(kp >= cand[None], kn >= cand), (Td, 1), n_keep)
        gt_p, eq_p = kp > thr[None], kp == thr[None]
        gt_n, eq_n = kn > thr, kn == thr
        n_gt = count(gt_p, gt_n)
        n_eq = count(eq_p, eq_n)
        bp_ref[0] = jnp.where(gt_p | eq_p, 0.0, NEG_BIAS)
        bn_ref[0] = jnp.where((gt_n | eq_n) & new_ok, 0.0, NEG_BIAS)
        tie = (thr > INT_MIN) & (n_gt + n_eq > n_keep)

        @pl.when(jnp.max(jnp.where(tie, 1, 0)) > 0)
        def _():
            need = n_keep - n_gt
            pos_p = (lax.broadcasted_iota(jnp.int32, (nchunk, Td, CH), 0) * CH
                     + lax.broadcasted_iota(jnp.int32, (nchunk, Td, CH), 2))
            pos_n = past + ncol
            nbits = (past + LANE).bit_length()

            def body(it, lim):
                cand = lim + lax.shift_left(jnp.int32(1), nbits - 1 - it)
                cnt = count(eq_p & (pos_p < cand[None]), eq_n & (pos_n < cand))
                return jnp.where(cnt <= need, cand, lim)

            lim = lax.fori_loop(0, nbits, body, jnp.zeros((Td, 1), jnp.int32))
            bp_ref[0] = jnp.where(gt_p | (eq_p & (pos_p < lim[None])), 0.0, NEG_BIAS)
            bn_ref[0] = jnp.where((gt_n | (eq_n & (pos_n < lim))) & new_ok, 0.0, NEG_BIAS)


def _dsa_decode_mask(page_table, qi_s, wcol, pool, knew, layer, Td):
    Bd, n_pages = page_table.shape
    page = pool.shape[2]
    pages = min(PAGES_PER_CHUNK, n_pages)
    assert n_pages % pages == 0
    nchunk = n_pages // pages
    CH = pages * page
    past = n_pages * page
    n_keep = min(DSA_TOPK, (past + Td) // 4)
    HT = qi_s.shape[1]
    kern = functools.partial(_dsa_decode_mask_kernel, layer=layer, nchunk=nchunk, pages=pages, page=page,
                             Td=Td, n_keep=n_keep, past=past)
    grid_spec = pltpu.PrefetchScalarGridSpec(
        num_scalar_prefetch=1, grid=(Bd, nchunk + 1),
        in_specs=[
            pl.BlockSpec((1, HT, IDX_DIM), lambda b, c, pt: (b, 0, 0)),
            pl.BlockSpec((1, HT, 1), lambda b, c, pt: (b, 0, 0)),
            pl.BlockSpec(memory_space=pl.ANY),
            pl.BlockSpec((1, LANE, IDX_DIM), lambda b, c, pt: (b, 0, 0)),
        ],
        out_specs=[
            pl.BlockSpec((1, nchunk, Td, CH), lambda b, c, pt: (b, 0, 0, 0)),
            pl.BlockSpec((1, Td, LANE), lambda b, c, pt: (b, 0, 0)),
        ],
        scratch_shapes=[
            pltpu.VMEM((2, CH, IDX_DIM), jnp.float32),
            pltpu.SemaphoreType.DMA((2,)),
            pltpu.VMEM((nchunk, Td, CH), jnp.float32),
            pltpu.VMEM((Td, LANE), jnp.float32),
        ])
    return pl.pallas_call(
        kern, grid_spec=grid_spec,
        out_shape=[jax.ShapeDtypeStruct((Bd, nchunk, Td, CH), jnp.float32),
                   jax.ShapeDtypeStruct((Bd, Td, LANE), jnp.float32)],
        compiler_params=_cparams("arbitrary", "arbitrary"), name="dsa_decode_mask",
    )(page_table, qi_s, wcol, pool, knew)


def _moba_decode_mask_kernel(pt_ref, q_ref, pool_ref, bp_ref, kbuf, sem, gate_ref,
                             *, layer, nchunk, pages, page, Td, rep, bs, n_pick):
    b = pl.program_id(0)
    c = pl.program_id(1)
    CH = pages * page
    per = CH // bs
    nblk = nchunk * per

    def start_fn(cc, slot):
        for cp in _page_copies(pt_ref, pool_ref, layer, kbuf, sem, b, cc, slot, pages, page, KV_B):
            cp.start()

    def wait_fn(cc, slot):
        for cp in _page_copies(pt_ref, pool_ref, layer, kbuf, sem, b, cc, slot, pages, page, KV_B):
            cp.wait()

    @pl.when(c == 0)
    def _():
        gate_ref[...] = jnp.full(gate_ref.shape, -jnp.inf, jnp.float32)

    slot = _paged_step(c, nchunk, start_fn, wait_fn)
    lane = lax.broadcasted_iota(jnp.int32, (Td, LANE), 1)
    for g in range(KV_B):
        qsum = jnp.sum(q_ref[0, g].astype(jnp.float32).reshape(rep, Td, HEAD_DIM), axis=0)
        gate = gate_ref[g]
        for nb in range(per):
            mean_nb = jnp.mean(kbuf[slot, g, nb * bs:(nb + 1) * bs, :], axis=0, keepdims=True)
            gate = jnp.where(lane == c * per + nb, jnp.sum(qsum * mean_nb, axis=-1, keepdims=True), gate)
        gate_ref[g] = gate

    @pl.when(c == nchunk - 1)
    def _():
        for g in range(KV_B):
            sel = jnp.where(lane < nblk, _topk_lanes(gate_ref[g], n_pick), 0.0)
            for n in range(nblk):
                picked = jnp.where(sel[:, n:n + 1] > 0.5, 0.0, NEG_BIAS)
                bp_ref[0, n // per, g, :, (n % per) * bs:(n % per + 1) * bs] = jnp.broadcast_to(picked, (Td, bs))


def _moba_decode_mask(page_table, q_s, pool, layer, Td, rep):
    Bd, n_pages = page_table.shape
    page = pool.shape[2]
    pages = min(PAGES_PER_CHUNK, n_pages)
    assert n_pages % pages == 0
    nchunk = n_pages // pages
    CH = pages * page
    bs = MOBA_BLOCK
    assert CH % bs == 0 and Td <= bs
    nblk = n_pages * page // bs
    assert nblk <= LANE
    kern = functools.partial(_moba_decode_mask_kernel, layer=layer, nchunk=nchunk, pages=pages, page=page,
                             Td=Td, rep=rep, bs=bs, n_pick=min(MOBA_TOPK, (n_pages * page + Td) // bs))
    grid_spec = pltpu.PrefetchScalarGridSpec(
        num_scalar_prefetch=1, grid=(Bd, nchunk),
        in_specs=[
            pl.BlockSpec((1, KV_B, rep * Td, HEAD_DIM), lambda b, c, pt: (b, 0, 0, 0)),
            pl.BlockSpec(memory_space=pl.ANY),
        ],
        out_specs=pl.BlockSpec((1, nchunk, KV_B, Td, CH), lambda b, c, pt: (b, 0, 0, 0, 0)),
        scratch_shapes=[
            pltpu.VMEM((2, KV_B, CH, HEAD_DIM), jnp.float32),
            pltpu.SemaphoreType.DMA((2,)),
            pltpu.VMEM((KV_B, Td, LANE), jnp.float32),
        ])
    return pl.pallas_call(
        kern, grid_spec=grid_spec,
        out_shape=jax.ShapeDtypeStruct((Bd, nchunk, KV_B, Td, CH), jnp.float32),
        compiler_params=_cparams("arbitrary", "arbitrary"), name="moba_decode_mask",
    )(page_table, q_s, pool)


def _paged_attn_kernel(pt_ref, q_ref, kpool_ref, vpool_ref, knew_ref, vnew_ref, bp_ref, bn_ref, o_ref,
                       kbuf, vbuf, sem, m_ref, l_ref, acc_ref,
                       *, layer, nchunk, pages, page, Td, rep, n_kv, per_group_bias, scale):
    b = pl.program_id(0)
    c = pl.program_id(1)

    def start_fn(cc, slot):
        for cp in (_page_copies(pt_ref, kpool_ref, layer, kbuf, sem.at[0], b, cc, slot, pages, page, n_kv)
                   + _page_copies(pt_ref, vpool_ref, layer, vbuf, sem.at[1], b, cc, slot, pages, page, n_kv)):
            cp.start()

    def wait_fn(cc, slot):
        for cp in (_page_copies(pt_ref, kpool_ref, layer, kbuf, sem.at[0], b, cc, slot, pages, page, n_kv)
                   + _page_copies(pt_ref, vpool_ref, layer, vbuf, sem.at[1], b, cc, slot, pages, page, n_kv)):
            cp.wait()

    @pl.when(c == 0)
    def _():
        m_ref[...] = jnp.full(m_ref.shape, -jnp.inf, jnp.float32)
        l_ref[...] = jnp.zeros(l_ref.shape, jnp.float32)
        acc_ref[...] = jnp.zeros(acc_ref.shape, jnp.float32)

    def process(k_fn, v_fn, bias_fn):
        for g in range(n_kv):
            k = k_fn(g).astype(MXU_DTYPE)
            v = v_fn(g).astype(MXU_DTYPE)
            bias = bias_fn(g)
            s = _dot_nt(q_ref[0, g], k) * scale + jnp.concatenate([bias] * rep, axis=0)
            m_old = m_ref[g]
            m_new = jnp.maximum(m_old, jnp.max(s, axis=-1, keepdims=True))
            a = jnp.exp(m_old - m_new)
            p = jnp.exp(s - m_new)
            l_ref[g] = a * l_ref[g] + jnp.sum(p, axis=-1, keepdims=True)
            acc_ref[g] = a * acc_ref[g] + jnp.dot(p.astype(MXU_DTYPE), v, preferred_element_type=jnp.float32)
            m_ref[g] = m_new

    @pl.when(c < nchunk)
    def _():
        slot = _paged_step(c, nchunk, start_fn, wait_fn)
        process(lambda g: kbuf[slot, g], lambda g: vbuf[slot, g],
                lambda g: bp_ref[0, 0, g if per_group_bias else 0])

    @pl.when(c == nchunk)
    def _():
        process(lambda g: knew_ref[0, :, g * HEAD_DIM:(g + 1) * HEAD_DIM],
                lambda g: vnew_ref[0, :, g * HEAD_DIM:(g + 1) * HEAD_DIM], lambda g: bn_ref[0])
        for g in range(n_kv):
            o_ref[0, g] = acc_ref[g] / l_ref[g]


def _paged_attention(page_table, q_s, kpool, vpool, knew, vnew, bias_past, bias_new, layer, Td, rep):
    Bd, n_pages = page_table.shape
    page = kpool.shape[2]
    pages = min(PAGES_PER_CHUNK, n_pages)
    nchunk = n_pages // pages
    CH = pages * page
    n_kv = q_s.shape[1]
    R = rep * Td
    W = n_kv * HEAD_DIM
    gb = bias_past.shape[2]
    nb_new = bias_new.shape[0]
    kern = functools.partial(_paged_attn_kernel, layer=layer, nchunk=nchunk, pages=pages, page=page, Td=Td,
                             rep=rep, n_kv=n_kv, per_group_bias=gb > 1, scale=HEAD_DIM ** -0.5)
    grid_spec = pltpu.PrefetchScalarGridSpec(
        num_scalar_prefetch=1, grid=(Bd, nchunk + 1),
        in_specs=[
            pl.BlockSpec((1, n_kv, R, HEAD_DIM), lambda b, c, pt: (b, 0, 0, 0)),
            pl.BlockSpec(memory_space=pl.ANY),
            pl.BlockSpec(memory_space=pl.ANY),
            pl.BlockSpec((1, LANE, W), lambda b, c, pt: (b, 0, 0)),
            pl.BlockSpec((1, LANE, W), lambda b, c, pt: (b, 0, 0)),
            pl.BlockSpec((1, 1, gb, Td, CH), lambda b, c, pt: (b, jnp.minimum(c, nchunk - 1), 0, 0, 0)),
            pl.BlockSpec((1, Td, LANE), lambda b, c, pt: (b if nb_new > 1 else 0, 0, 0)),
        ],
        out_specs=pl.BlockSpec((1, n_kv, R, HEAD_DIM), lambda b, c, pt: (b, 0, 0, 0)),
        scratch_shapes=[
            pltpu.VMEM((2, n_kv, CH, HEAD_DIM), jnp.float32),
            pltpu.VMEM((2, n_kv, CH, HEAD_DIM), jnp.float32),
            pltpu.SemaphoreType.DMA((2, 2)),
            pltpu.VMEM((n_kv, R, 1), jnp.float32),
            pltpu.VMEM((n_kv, R, 1), jnp.float32),
            pltpu.VMEM((n_kv, R, HEAD_DIM), jnp.float32),
        ])
    return pl.pallas_call(
        kern, grid_spec=grid_spec,
        out_shape=jax.ShapeDtypeStruct((Bd, n_kv, R, HEAD_DIM), jnp.float32),
        compiler_params=_cparams("arbitrary", "arbitrary"), name="paged_attention",
    )(page_table, q_s, kpool, vpool, knew, vnew, bias_past, bias_new)


def _route_kernel(lg_ref, eid_ref, gate_ref):
    lg = lg_ref[...]
    lane = lax.broadcasted_iota(jnp.int32, lg.shape, 1)
    gmask = lane < N_GROUPS
    gl = jnp.where(gmask, lg, -jnp.inf)
    gmax = jnp.max(gl, axis=-1, keepdims=True)
    gsel = jnp.min(jnp.where(gl == gmax, lane, LANE), axis=-1, keepdims=True)
    g_w = 1.0 / jnp.sum(jnp.where(gmask, jnp.exp(gl - gmax), 0.0), axis=-1, keepdims=True)
    lo = N_GROUPS + gsel * EXPERTS_PER_GROUP
    el = jnp.where((lane >= lo) & (lane < lo + EXPERTS_PER_GROUP), lg, -jnp.inf)
    v1 = jnp.max(el, axis=-1, keepdims=True)
    i1 = jnp.min(jnp.where(el == v1, lane, LANE), axis=-1, keepdims=True)
    el2 = jnp.where(lane == i1, -jnp.inf, el)
    v2 = jnp.max(el2, axis=-1, keepdims=True)
    i2 = jnp.min(jnp.where(el2 == v2, lane, LANE), axis=-1, keepdims=True)
    e2 = jnp.exp(v2 - v1)
    p1 = 1.0 / (1.0 + e2)
    eid_ref[...] = jnp.where(lane == 0, i1 - N_GROUPS, jnp.where(lane == 1, i2 - N_GROUPS, 0))
    gate_ref[...] = jnp.where(lane == 0, g_w * p1, jnp.where(lane == 1, g_w * (e2 * p1), 0.0))


def _route(logits):
    Np = logits.shape[0]
    tr = _pick_tile(Np, 1024, 8)
    spec = pl.BlockSpec((tr, LANE), lambda i: (i, 0))
    return pl.pallas_call(
        _route_kernel, grid=(Np // tr,), in_specs=[spec], out_specs=[spec, spec],
        out_shape=[jax.ShapeDtypeStruct((Np, LANE), jnp.int32), jax.ShapeDtypeStruct((Np, LANE), jnp.float32)],
        compiler_params=_cparams("parallel"), name="route")(logits)


def _moe_plan(eid, n_experts, rc, nch_max):
    M = eid.shape[0]
    C = rc * nch_max
    order = jnp.argsort(eid, stable=True).astype(jnp.int32)
    eid_s = eid[order]
    counts = jnp.bincount(eid, length=n_experts).astype(jnp.int32)
    start = jnp.cumsum(counts) - counts
    pcounts = (counts + rc - 1) // rc * rc
    pstart = jnp.cumsum(pcounts) - pcounts
    slot_s = (pstart[eid_s] + jnp.arange(M, dtype=jnp.int32) - start[eid_s]).astype(jnp.int32)
    R = _round_up(M, rc) + n_experts * rc
    pend = pstart + pcounts
    slots = jnp.arange(R, dtype=jnp.int32)
    e_slot = jnp.minimum(jnp.sum(pend[None, :] <= slots[:, None], axis=1), n_experts - 1)
    j_slot = slots - pstart[e_slot]
    src = jnp.clip(start[e_slot] + j_slot, 0, M - 1)
    tok = jnp.where((j_slot < counts[e_slot]) & (slots < pend[-1]), order[src] // EXPERT_TOPK, 0)
    dest = slot_s[jnp.argsort(order)]
    nseg = (pcounts + C - 1) // C
    send = jnp.cumsum(nseg)
    sstart = send - nseg
    n_seg = -(-R // C) + n_experts
    sidx = jnp.arange(n_seg, dtype=jnp.int32)
    e_of = jnp.minimum(jnp.sum(send[None, :] <= sidx[:, None], axis=1), n_experts - 1).astype(jnp.int32)
    active = sidx < send[-1]
    local = sidx - sstart[e_of]
    nch = jnp.where(active, jnp.clip((pcounts[e_of] - local * C + rc - 1) // rc, 0, nch_max), 0)
    used = pstart[-1] + pcounts[-1]
    idle_row0 = used + (sidx - send[-1]) * C
    row0 = jnp.where(active, pstart[e_of] + local * C, jnp.minimum(idle_row0, R))
    nzero = jnp.where(active, 0, jnp.clip((R - idle_row0) // rc, 0, nch_max))
    seg_e = jnp.where(active, e_of, e_of[jnp.maximum(send[-1] - 1, 0)])
    i32 = jnp.int32
    return tok, dest, seg_e.astype(i32), row0.astype(i32), nch.astype(i32), nzero.astype(i32), R


def _moe_kernel(seg_e_ref, row0_ref, nch_ref, nzero_ref, tok_ref, x_hbm, wg_ref, wu_ref, wd_ref, y_hbm,
                xp, a_acc, u_acc, hbuf, wgb, wub, wdb, ostage, gsem, osem, *, rc, nk, nn, n_seg):
    del seg_e_ref
    s = pl.program_id(0)
    j = pl.program_id(1)
    nch = nch_ref[s]
    row0 = row0_ref[s]
    nzero = nzero_ref[s]
    cur = s % 2
    kt = wgb.shape[0]

    def chunk_rows(ch):
        return pl.ds(pl.multiple_of(ch * rc, rc), rc)

    def out_copy(slot, ch, n):
        dst = y_hbm.at[pl.ds(pl.multiple_of(row0 + ch * rc, rc), rc), n, :]
        return pltpu.make_async_copy(ostage.at[slot], dst, osem.at[slot])

    @pl.when((j == 0) & (nzero > 0))
    def _():
        ostage[0] = jnp.zeros(ostage.shape[1:], jnp.float32)

        def start(ch, carry):
            for n in range(nn):
                out_copy(0, ch, n).start()
            return carry

        def wait(ch, carry):
            for n in range(nn):
                out_copy(0, ch, n).wait()
            return carry
        lax.fori_loop(0, nzero, start, 0)
        lax.fori_loop(0, nzero, wait, 0)

    def gather(seg, buf, wait):
        base = row0_ref[seg]

        def body(r8, carry):
            for u in range(8):
                r = r8 * 8 + u
                cp = pltpu.make_async_copy(x_hbm.at[pl.ds(tok_ref[base + r], 1)], xp.at[buf, pl.ds(r, 1)],
                                           gsem.at[buf])
                if wait:
                    cp.wait()
                else:
                    cp.start(priority=GATHER_DMA_PRIORITY)
            return carry
        lax.fori_loop(0, nch_ref[seg] * (rc // 8), body, 0)

    @pl.when((j == 0) & (nch > 0))
    def _():
        @pl.when(s == 0)
        def _():
            gather(0, 0, False)

        gather(s, cur, True)
        nxt = jnp.minimum(s + 1, n_seg - 1)

        @pl.when((s + 1 < n_seg) & (nch_ref[nxt] > 0))
        def _():
            gather(nxt, 1 - cur, False)

    @pl.when((j < nk) & (nch > 0))
    def _():
        wgb[...] = wg_ref[...].astype(wgb.dtype)
        wub[...] = wu_ref[...].astype(wub.dtype)

    for jj in range(nk):
        @pl.when((j == jj) & (nch > 0))
        def _(jj=jj):
            def body(ch, carry):
                rows = chunk_rows(ch)
                w = xp[cur, rows, (jj // 2) * kt:(jj // 2 + 1) * kt]
                if jj % 2 == 0:
                    xf = lax.bitcast_convert_type(lax.shift_left(w, jnp.uint32(16)), jnp.float32)
                else:
                    xf = lax.bitcast_convert_type(w & jnp.uint32(0xFFFF0000), jnp.float32)
                x = xf.astype(wgb.dtype)
                a = jnp.dot(x, wgb[...], preferred_element_type=jnp.float32)
                u = jnp.dot(x, wub[...], preferred_element_type=jnp.float32)
                if jj == 0:
                    a_acc[rows, :] = a
                    u_acc[rows, :] = u
                else:
                    a_acc[rows, :] += a
                    u_acc[rows, :] += u
                return carry
            lax.fori_loop(0, nch, body, 0)

    @pl.when((j == nk - 1) & (nch > 0))
    def _():
        def body(ch, carry):
            rows = chunk_rows(ch)
            a = a_acc[rows, :]
            hbuf[rows, :] = (a * jax.nn.sigmoid(a) * u_acc[rows, :]).astype(hbuf.dtype)
            return carry
        lax.fori_loop(0, nch, body, 0)

    @pl.when((j >= nk) & (nch > 0))
    def _():
        wdb[...] = wd_ref[...].astype(wdb.dtype)

    for n in range(nn):
        @pl.when((j == nk + n) & (nch > 0))
        def _(n=n):
            def body(ch, carry):
                slot = ch % 2

                @pl.when(ch >= 2)
                def _():
                    out_copy(slot, ch - 2, n).wait()

                ostage[slot] = jnp.dot(hbuf[chunk_rows(ch), :], wdb[...], preferred_element_type=jnp.float32)
                out_copy(slot, ch, n).start()
                return carry
            lax.fori_loop(0, nch, body, 0)

            @pl.when(nch >= 2)
            def _():
                out_copy(nch % 2, nch - 2, n).wait()

            out_copy((nch - 1) % 2, nch - 1, n).wait()


def _moe_experts(xpk, kt, plan, w_gate, w_up, w_down, layer):
    tok, _, seg_e, row0, nch, nzero, R = plan
    D = 2 * xpk.shape[1]
    nk = D // kt
    F = w_gate.shape[-1]
    nt = min(MOE_NT, D)
    nn = D // nt
    rc = MOE_ROWS
    C = rc * MOE_CHUNKS
    n_seg = seg_e.shape[0]

    def k_idx(s, j, nc):
        return jnp.where(nc[s] > 0, jnp.minimum(j, nk - 1), nk - 1)

    def n_idx(s, j, nc):
        return jnp.where(nc[s] > 0, jnp.maximum(j - nk, 0), nn - 1)

    grid_spec = pltpu.PrefetchScalarGridSpec(
        num_scalar_prefetch=5, grid=(n_seg, nk + nn),
        in_specs=[
            pl.BlockSpec(memory_space=pl.ANY),
            pl.BlockSpec((None, None, kt, F), lambda s, j, se, r0, nc, nz, tk: (layer, se[s], k_idx(s, j, nc), 0)),
            pl.BlockSpec((None, None, kt, F), lambda s, j, se, r0, nc, nz, tk: (layer, se[s], k_idx(s, j, nc), 0)),
            pl.BlockSpec((None, None, F, nt), lambda s, j, se, r0, nc, nz, tk: (layer, se[s], 0, n_idx(s, j, nc))),
        ],
        out_specs=pl.BlockSpec(memory_space=pl.ANY),
        scratch_shapes=[
            pltpu.VMEM((2, C, D // 2), jnp.uint32),
            pltpu.VMEM((C, F), jnp.float32),
            pltpu.VMEM((C, F), jnp.float32),
            pltpu.VMEM((C, F), MXU_DTYPE),
            pltpu.VMEM((kt, F), MXU_DTYPE),
            pltpu.VMEM((kt, F), MXU_DTYPE),
            pltpu.VMEM((F, nt), MXU_DTYPE),
            pltpu.VMEM((2, rc, nt), jnp.float32),
            pltpu.SemaphoreType.DMA((2,)),
            pltpu.SemaphoreType.DMA((2,)),
        ])
    return pl.pallas_call(
        functools.partial(_moe_kernel, rc=rc, nk=nk, nn=nn, n_seg=n_seg), grid_spec=grid_spec,
        out_shape=jax.ShapeDtypeStruct((R, nn, nt), jnp.float32),
        compiler_params=_cparams("arbitrary", "arbitrary"), name="moe_experts",
    )(seg_e, row0, nch, nzero, tok, xpk, w_gate, w_up, w_down)


def _combine_kernel(dest_ref, x_ref, gate_ref, gam_ref, bet_ref, y_hbm, o32_ref, o16_ref, ybuf, sem,
                    *, tc, alpha, n_tiles):
    i = pl.program_id(0)
    cur = i % 2

    def rows(tile, buf, wait):
        def body(r, carry):
            for k in range(EXPERT_TOPK):
                slot = dest_ref[(tile * tc + r) * EXPERT_TOPK + k]
                cp = pltpu.make_async_copy(y_hbm.at[slot], ybuf.at[buf, k, r], sem.at[buf])
                if wait:
                    cp.wait()
                else:
                    cp.start(priority=GATHER_DMA_PRIORITY)
            return carry
        lax.fori_loop(0, tc, body, 0)

    @pl.when(i == 0)
    def _():
        rows(0, 0, False)

    rows(i, cur, True)

    @pl.when(i + 1 < n_tiles)
    def _():
        rows(i + 1, 1 - cur, False)

    gate = gate_ref[...]
    nn, nt = ybuf.shape[3], ybuf.shape[4]
    parts = []
    for n in range(nn):
        ffn = gate[:, 0:1] * ybuf[cur, 0, :, n, :]
        for k in range(1, EXPERT_TOPK):
            ffn = ffn + gate[:, k:k + 1] * ybuf[cur, k, :, n, :]
        parts.append(alpha * x_ref[:, n * nt:(n + 1) * nt] + ffn)
    v = jnp.concatenate(parts, axis=1)
    mu = jnp.mean(v, axis=-1, keepdims=True)
    d = v - mu
    var = jnp.mean(d * d, axis=-1, keepdims=True)
    y = d * lax.rsqrt(var + LN_EPS) * gam_ref[...] + bet_ref[...]
    o32_ref[...] = y
    o16_ref[...] = y.astype(o16_ref.dtype)


def _moe_combine(dest, x32, gates, y_sorted, gam, bet, alpha):
    Np, D = x32.shape
    tc = _pick_tile(Np, 192, 16)
    row = lambda i, d: (i, 0)
    grid_spec = pltpu.PrefetchScalarGridSpec(
        num_scalar_prefetch=1, grid=(Np // tc,),
        in_specs=[
            pl.BlockSpec((tc, D), row),
            pl.BlockSpec((tc, LANE), row),
            pl.BlockSpec((1, D), lambda i, d: (0, 0)),
            pl.BlockSpec((1, D), lambda i, d: (0, 0)),
            pl.BlockSpec(memory_space=pl.ANY),
        ],
        out_specs=[pl.BlockSpec((tc, D), row), pl.BlockSpec((tc, D), row)],
        scratch_shapes=[pltpu.VMEM((2, EXPERT_TOPK, tc) + y_sorted.shape[1:], jnp.float32),
                        pltpu.SemaphoreType.DMA((2,))])
    return pl.pallas_call(
        functools.partial(_combine_kernel, tc=tc, alpha=alpha, n_tiles=Np // tc), grid_spec=grid_spec,
        out_shape=[jax.ShapeDtypeStruct((Np, D), jnp.float32), jax.ShapeDtypeStruct((Np, D), MXU_DTYPE)],
        compiler_params=_cparams("arbitrary"), name="moe_combine",
    )(dest, x32, gates, gam.reshape(1, D), bet.reshape(1, D), y_sorted)


def _rope_tables(pos, scale_wi):
    def cs(dim):
        half = dim // 2
        inv = ROPE_THETA ** (-jnp.arange(half, dtype=jnp.float32) * 2.0 / dim)
        ang = pos.astype(jnp.float32)[:, None] * inv[None, :]
        return jnp.cos(ang), jnp.sin(ang)
    c, s = cs(HEAD_DIM)
    t128 = (jnp.concatenate([c, c], -1), jnp.concatenate([-s, s], -1))
    c, s = cs(IDX_DIM)
    c64 = jnp.concatenate([c, c], -1)
    s64 = jnp.concatenate([-s, s], -1)
    t64 = (jnp.tile(c64, (1, LANE // IDX_DIM)), jnp.tile(s64, (1, LANE // IDX_DIM)))
    n = pos.shape[0]
    pad = LANE - IDX_DIM - IDX_HEADS
    tkw = (jnp.concatenate([c64, jnp.full((n, IDX_HEADS), scale_wi, jnp.float32), jnp.zeros((n, pad), jnp.float32)], -1),
           jnp.concatenate([s64, jnp.zeros((n, LANE - IDX_DIM), jnp.float32)], -1))
    return t128, t64, tkw


def kernel(x_prompt, x_sample, cache_a_k, cache_a_v, cache_a_kidx, cache_b_k, cache_b_v, page_table, p_prompt, p_sample, ln_emb_g, ln_emb_b, w_in, w_out, ln1_g, ln1_b, w_route_group, w_route_expert, w_exp_gate, w_exp_up, w_exp_down, ln2_g, ln2_b, w_ple, w_ple_gate, ln3_g, ln3_b):
    B, T, D = x_prompt.shape
    Bd, Td, _ = x_sample.shape
    depth = w_in.shape[0]
    n_phys, page = cache_a_k.shape[1], cache_a_k.shape[2]
    past = page_table.shape[1] * page
    n_experts = w_exp_gate.shape[1]
    h_a = D // 2 // HEAD_DIM
    h_b = h_a
    rep_a, rep_b = h_a // KV_A, h_b // KV_B
    assert IDX_DIM * 2 == LANE and IDX_DIM + IDX_HEADS <= LANE and page == LANE
    assert N_GROUPS * (1 + EXPERTS_PER_GROUP) <= LANE and n_experts == N_GROUPS * EXPERTS_PER_GROUP
    alpha = (2.0 * depth) ** 0.25
    n_p, n_s = B * T, Bd * Td
    N = n_p + n_s
    Np = _round_up(N, ROW_ALIGN)
    f32 = jnp.float32

    def stream(a_p, a_s):
        w = a_p.shape[-1]
        return jnp.concatenate([a_p.reshape(n_p, w), a_s.reshape(n_s, w), jnp.zeros((Np - N, w), a_p.dtype)], 0)

    pos = jnp.concatenate([jnp.tile(jnp.arange(T, dtype=jnp.int32), B),
                           jnp.tile(past + jnp.arange(Td, dtype=jnp.int32), Bd),
                           jnp.zeros((Np - N,), jnp.int32)])
    t128, t64, tkw = _rope_tables(pos, IDX_HEADS ** -0.5 * IDX_DIM ** -0.5)

    widths = (h_a * HEAD_DIM, KV_A * HEAD_DIM, KV_A * HEAD_DIM, IDX_HEADS * IDX_DIM, IDX_DIM, IDX_HEADS,
              h_b * HEAD_DIM, KV_B * HEAD_DIM, KV_B * HEAD_DIM)
    offs = [0]
    for w in widths:
        offs.append(offs[-1] + w)
    assert offs[-1] == w_in.shape[2]

    def cols(w, *ids):
        return jnp.concatenate([w[:, offs[i]:offs[i + 1]] for i in ids], axis=1).astype(MXU_DTYPE)

    kv_w = KV_A * HEAD_DIM
    trow = jnp.arange(Td, dtype=jnp.int32)[:, None]
    own_bias = jnp.where(jnp.arange(LANE, dtype=jnp.int32)[None, :] <= trow, 0.0, NEG_BIAS).astype(f32)[None]

    def fresh(rows):
        w = rows.shape[-1]
        return jnp.pad(rows.reshape(Bd, Td, w), ((0, 0), (0, LANE - Td), (0, 0)))

    def decode_q(q_sm, kv, rep):
        q = q_sm.reshape(kv, rep, Bd, Td, HEAD_DIM)
        return jnp.transpose(q, (2, 0, 1, 3, 4)).reshape(Bd, kv, rep * Td, HEAD_DIM)

    def decode_o(o, kv, rep):
        o = o.reshape(Bd, kv, rep, Td, HEAD_DIM)
        return jnp.transpose(o, (0, 3, 1, 2, 4)).reshape(n_s, kv * rep * HEAD_DIM)

    x_tail = jnp.concatenate([x_sample.reshape(n_s, D), jnp.zeros((Np - N, D), x_sample.dtype)], axis=0)
    h32, h16 = _layer_norm_embed(x_prompt.reshape(n_p, D), x_tail, ln_emb_g, ln_emb_b)
    rows_out = []
    for l in range(depth):
        wl = w_in[l]
        q128 = _matmul([h16], [cols(wl, 0, 6)], MXU_DTYPE, "rope128", t128, True, name="proj_q")
        k128 = _matmul([h16], [cols(wl, 1, 7)], f32, "rope128", t128, name="proj_k")
        v128 = _matmul([h16], [cols(wl, 2, 8)], f32, name="proj_v")
        qi = _matmul([h16], [cols(wl, 3)], MXU_DTYPE, "rope64", t64, True, name="proj_qi")
        w_kw = jnp.pad(cols(wl, 4, 5), ((0, 0), (0, LANE - IDX_DIM - IDX_HEADS)))
        kiwi = _matmul([h16], [w_kw], f32, "rope64", tkw, name="proj_kiwi")

        attn_a = _dsa_prompt(q128, k128, v128, qi, kiwi, B, T, rep_a)
        attn_b = _moba_prompt(q128, k128, v128, B, T, rep_b, h_a)

        ks, vs = k128[n_p:N], v128[n_p:N]
        kiwi_s = kiwi[n_p:N]
        qi_s = qi[:, n_p:N].reshape(IDX_HEADS // 2, Bd, Td, 2, IDX_DIM)
        qi_s = jnp.transpose(qi_s, (1, 0, 3, 2, 4)).reshape(Bd, IDX_HEADS * Td, IDX_DIM)
        wcol = jnp.transpose(kiwi_s[:, IDX_DIM:IDX_DIM + IDX_HEADS].reshape(Bd, Td, IDX_HEADS), (0, 2, 1))
        wcol = wcol.reshape(Bd, IDX_HEADS * Td, 1)
        bias_p, bias_n = _dsa_decode_mask(page_table, qi_s, wcol, cache_a_kidx, fresh(kiwi_s[:, :IDX_DIM]), l, Td)
        o_a = _paged_attention(page_table, decode_q(q128[:h_a, n_p:N], KV_A, rep_a), cache_a_k, cache_a_v,
                               fresh(ks[:, :kv_w]), fresh(vs[:, :kv_w]), bias_p[:, :, None], bias_n, l, Td, rep_a)
        qb_s = decode_q(q128[h_a:, n_p:N], KV_B, rep_b)
        bias_b = _moba_decode_mask(page_table, qb_s, cache_b_k, l, Td, rep_b)
        o_b = _paged_attention(page_table, qb_s, cache_b_k, cache_b_v, fresh(ks[:, kv_w:]), fresh(vs[:, kv_w:]),
                               bias_b, own_bias, l, Td, rep_b)
        tail = jnp.zeros((Np - N, h_a * HEAD_DIM), MXU_DTYPE)
        attn_a = jnp.concatenate([attn_a, decode_o(o_a, KV_A, rep_a).astype(MXU_DTYPE), tail], axis=0)
        attn_b = jnp.concatenate([attn_b, decode_o(o_b, KV_B, rep_b).astype(MXU_DTYPE), tail], axis=0)

        w_o = w_out[l].astype(MXU_DTYPE)
        mix = _matmul([attn_a, attn_b], [w_o[:h_a * HEAD_DIM], w_o[h_a * HEAD_DIM:]], f32, name="proj_out")
        w_r = jnp.pad(jnp.concatenate([w_route_group[l], w_route_expert[l]], axis=1),
                      ((0, 0), (0, LANE - N_GROUPS - n_experts)))
        w_rh = w_r.astype(MXU_DTYPE)
        w_rl = (w_r - w_rh.astype(f32)).astype(MXU_DTYPE)
        moe_kt = min(MOE_KT, D // 2)
        x32, xpk, logits = _layer_norm([h32, mix], ln1_g[l], ln1_b[l], "add", alpha, (w_rh, w_rl), pack_kt=moe_kt)

        eid, gates = _route(logits)
        plan = _moe_plan(eid[:N, :EXPERT_TOPK].reshape(-1), n_experts, MOE_ROWS, MOE_CHUNKS)
        y_sorted = _moe_experts(xpk, moe_kt, plan, w_exp_gate, w_exp_up, w_exp_down, l)
        dest = jnp.pad(plan[1], (0, (Np - N) * EXPERT_TOPK))
        x32, x16 = _moe_combine(dest, x32, gates, y_sorted, ln2_g[l], ln2_b[l], alpha)

        gate_pre = _matmul([x16], [w_ple_gate[l].astype(MXU_DTYPE)], f32, name="ple_gate")
        p16 = stream(p_prompt[l], p_sample[l]).astype(MXU_DTYPE)
        ple = _matmul([p16], [w_ple[l].astype(MXU_DTYPE)], f32, name="ple_embed")
        h32, h16 = _layer_norm([x32, gate_pre, ple], ln3_g[l], ln3_b[l], "ple", alpha)
        rows_out.append((k128, v128, kiwi))

    def gather_rows(sel, lo, hi, lead):
        return jnp.stack([sel(r)[lo:hi].reshape(lead) for r in rows_out])

    outs = [h32[:n_p].reshape(B, T, D), h32[n_p:N].reshape(Bd, Td, D)]
    for lo, hi, lead in ((0, n_p, (B, T)), (n_p, N, (Bd, Td))):
        outs += [
            gather_rows(lambda r: r[0][:, :kv_w], lo, hi, lead + (KV_A, HEAD_DIM)),
            gather_rows(lambda r: r[1][:, :kv_w], lo, hi, lead + (KV_A, HEAD_DIM)),
            gather_rows(lambda r: r[2][:, :IDX_DIM], lo, hi, lead + (IDX_DIM,)),
            gather_rows(lambda r: r[0][:, kv_w:], lo, hi, lead + (KV_B, HEAD_DIM)),
            gather_rows(lambda r: r[1][:, kv_w:], lo, hi, lead + (KV_B, HEAD_DIM)),
        ]
    return tuple(outs)
```

```python
import functools

import jax
import jax.numpy as jnp
from jax import lax
from jax.experimental import pallas as pl
from jax.experimental.pallas import tpu as pltpu

HEAD_DIM = 128
KV_A = 4
IDX_HEADS = 32
IDX_DIM = 64
DSA_TOPK = 256
KV_B = 4
MOBA_BLOCK = 256
MOBA_TOPK = 3
N_GROUPS = 4
EXPERTS_PER_GROUP = 8
EXPERT_TOPK = 2
ROPE_THETA = 10000.0
LN_EPS = 1e-5

LANE = 128
ROW_ALIGN = 256
MXU_DTYPE = jnp.bfloat16
NEG_BIAS = -1e30
INT_MIN = -2147483648
VMEM_LIMIT = 56 * 1024 * 1024
PAGES_PER_CHUNK = 16
MOE_ROWS = 128
MOE_CHUNKS = 5
GATHER_DMA_PRIORITY = 1
MOE_KT = 1024
MOE_NT = 1024


def _cparams(*sem):
    return pltpu.CompilerParams(dimension_semantics=sem, vmem_limit_bytes=VMEM_LIMIT)


def _round_up(n, m):
    return (n + m - 1) // m * m


def _pick_tile(n, cap, mult):
    best = None
    for t in range(mult, cap + 1, mult):
        if n % t == 0:
            best = t
    assert best is not None, (n, cap, mult)
    return best


def _dot_nt(a, b):
    return lax.dot_general(a, b, (((1,), (1,)), ((), ())), preferred_element_type=jnp.float32)


def _ln_kernel(*refs, mode, alpha, router, pack_kt):
    it = iter(refs)
    x_ref = next(it)
    a_ref = next(it) if mode == "add" else None
    g_ref = next(it) if mode == "ple" else None
    p_ref = next(it) if mode == "ple" else None
    gam_ref, bet_ref = next(it), next(it)
    wh_ref = next(it) if router else None
    wl_ref = next(it) if router else None
    o32_ref, o16_ref = next(it), next(it)
    lg_ref = next(it) if router else None

    v = x_ref[...]
    if mode == "add":
        v = alpha * v + a_ref[...]
    elif mode == "ple":
        v = alpha * v + jax.nn.sigmoid(g_ref[...]) * p_ref[...]
    mu = jnp.mean(v, axis=-1, keepdims=True)
    d = v - mu
    var = jnp.mean(d * d, axis=-1, keepdims=True)
    y = d * lax.rsqrt(var + LN_EPS) * gam_ref[...] + bet_ref[...]
    o32_ref[...] = y
    if pack_kt is not None:
        kt = pack_kt
        for b in range(o16_ref.shape[1] // kt):
            lo = y[:, 2 * b * kt:(2 * b + 1) * kt].astype(jnp.bfloat16).astype(jnp.float32)
            hi = y[:, (2 * b + 1) * kt:(2 * b + 2) * kt].astype(jnp.bfloat16).astype(jnp.float32)
            o16_ref[:, b * kt:(b + 1) * kt] = (
                lax.shift_right_logical(lax.bitcast_convert_type(lo, jnp.uint32), jnp.uint32(16))
                | (lax.bitcast_convert_type(hi, jnp.uint32) & jnp.uint32(0xFFFF0000)))
    else:
        o16_ref[...] = y.astype(o16_ref.dtype)
    if router:
        yh = y.astype(MXU_DTYPE)
        yl = (y - yh.astype(jnp.float32)).astype(MXU_DTYPE)
        wh = wh_ref[...]
        lg = jnp.dot(yh, wh, preferred_element_type=jnp.float32)
        lg = lg + jnp.dot(yl, wh, preferred_element_type=jnp.float32)
        lg = lg + jnp.dot(yh, wl_ref[...], preferred_element_type=jnp.float32)
        lg_ref[...] = lg


def _ln_embed_kernel(xh_ref, xt_ref, gam_ref, bet_ref, o32_ref, o16_ref, *, n_head):
    v = jnp.where(pl.program_id(0) < n_head, xh_ref[...], xt_ref[...])
    mu = jnp.mean(v, axis=-1, keepdims=True)
    d = v - mu
    var = jnp.mean(d * d, axis=-1, keepdims=True)
    y = d * lax.rsqrt(var + LN_EPS) * gam_ref[...] + bet_ref[...]
    o32_ref[...] = y
    o16_ref[...] = y.astype(o16_ref.dtype)


def _layer_norm_embed(x_head, x_tail, gam, bet):
    n_h, D = x_head.shape
    n_t = x_tail.shape[0]
    tr = _pick_tile(ROW_ALIGN, 128, 16)
    assert n_h % tr == 0 and n_t % tr == 0
    nh, ntl = n_h // tr, n_t // tr
    vec = pl.BlockSpec((1, D), lambda i: (0, 0))
    row = pl.BlockSpec((tr, D), lambda i: (i, 0))
    return pl.pallas_call(
        functools.partial(_ln_embed_kernel, n_head=nh), grid=(nh + ntl,),
        in_specs=[pl.BlockSpec((tr, D), lambda i: (jnp.minimum(i, nh - 1), 0)),
                  pl.BlockSpec((tr, D), lambda i: (jnp.maximum(i - nh, 0), 0)), vec, vec],
        out_specs=[row, row],
        out_shape=[jax.ShapeDtypeStruct((n_h + n_t, D), jnp.float32),
                   jax.ShapeDtypeStruct((n_h + n_t, D), MXU_DTYPE)],
        compiler_params=_cparams("parallel"), name="ln_embed")(x_head, x_tail, gam.reshape(1, D), bet.reshape(1, D))


def _layer_norm(xs, gam, bet, mode, alpha=1.0, router_w=None, pack_kt=None):
    Np, D = xs[0].shape
    tr = _pick_tile(Np, 192, 16)
    row = pl.BlockSpec((tr, D), lambda i: (i, 0))
    vec = pl.BlockSpec((1, D), lambda i: (0, 0))
    in_specs = [row] * len(xs) + [vec, vec]
    args = list(xs) + [gam.reshape(1, D), bet.reshape(1, D)]
    if pack_kt is None:
        out_shape = [jax.ShapeDtypeStruct((Np, D), jnp.float32), jax.ShapeDtypeStruct((Np, D), MXU_DTYPE)]
        out_specs = [row, row]
    else:
        assert D % (2 * pack_kt) == 0
        out_shape = [jax.ShapeDtypeStruct((Np, D), jnp.float32), jax.ShapeDtypeStruct((Np, D // 2), jnp.uint32)]
        out_specs = [row, pl.BlockSpec((tr, D // 2), lambda i: (i, 0))]
    if router_w is not None:
        wspec = pl.BlockSpec((D, LANE), lambda i: (0, 0))
        in_specs += [wspec, wspec]
        args += list(router_w)
        out_shape.append(jax.ShapeDtypeStruct((Np, LANE), jnp.float32))
        out_specs.append(pl.BlockSpec((tr, LANE), lambda i: (i, 0)))
    return pl.pallas_call(
        functools.partial(_ln_kernel, mode=mode, alpha=alpha, router=router_w is not None, pack_kt=pack_kt),
        grid=(Np // tr,), in_specs=in_specs, out_specs=out_specs, out_shape=out_shape,
        compiler_params=_cparams("parallel"), name="ln_" + mode)(*args)


def _mm_kernel(*refs, n_x, mode, slice_major):
    x_refs = refs[:n_x]
    w_refs = refs[n_x:2 * n_x]
    rest = refs[2 * n_x:]
    if mode == "none":
        (o_ref,) = rest
    else:
        cos_ref, sin_ref, o_ref = rest
    y = jnp.dot(x_refs[0][...], w_refs[0][...], preferred_element_type=jnp.float32)
    for k in range(1, n_x):
        y = y + jnp.dot(x_refs[k][...], w_refs[k][...], preferred_element_type=jnp.float32)
    if mode == "none" and not slice_major:
        o_ref[...] = y.astype(o_ref.dtype)
        return
    tm = y.shape[0]
    if mode != "none":
        cos = cos_ref[...]
        sin = sin_ref[...]
    if mode == "rope64":
        lane = lax.broadcasted_iota(jnp.int32, (tm, LANE), 1)
        first_half = (lane % 64) < 32
    for s in range(y.shape[1] // LANE):
        yh = y[:, s * LANE:(s + 1) * LANE]
        if mode == "rope128":
            yh = yh * cos + pltpu.roll(yh, 64, 1) * sin
        elif mode == "rope64":
            partner = jnp.where(first_half, pltpu.roll(yh, 96, 1), pltpu.roll(yh, 32, 1))
            yh = yh * cos + partner * sin
        if slice_major:
            o_ref[s] = yh.astype(o_ref.dtype)
        else:
            o_ref[:, s * LANE:(s + 1) * LANE] = yh.astype(o_ref.dtype)


def _matmul(xs, ws, out_dtype, mode="none", tables=None, slice_major=False, name="mm"):
    Np = xs[0].shape[0]
    Nc = ws[0].shape[1]
    tm = _pick_tile(Np, 640, 16)
    tn = _pick_tile(Nc, 1024, LANE)
    in_specs = [pl.BlockSpec((tm, x.shape[1]), lambda j, i: (i, 0)) for x in xs]
    in_specs += [pl.BlockSpec((w.shape[0], tn), lambda j, i: (0, j)) for w in ws]
    args = list(xs) + list(ws)
    if mode != "none":
        in_specs += [pl.BlockSpec((tm, LANE), lambda j, i: (i, 0))] * 2
        args += list(tables)
    if slice_major:
        out_shape = jax.ShapeDtypeStruct((Nc // LANE, Np, LANE), out_dtype)
        out_spec = pl.BlockSpec((tn // LANE, tm, LANE), lambda j, i: (j, i, 0))
    else:
        out_shape = jax.ShapeDtypeStruct((Np, Nc), out_dtype)
        out_spec = pl.BlockSpec((tm, tn), lambda j, i: (i, j))
    return pl.pallas_call(
        functools.partial(_mm_kernel, n_x=len(xs), mode=mode, slice_major=slice_major),
        grid=(Nc // tn, Np // tm), in_specs=in_specs, out_specs=out_spec, out_shape=out_shape,
        compiler_params=_cparams("parallel", "parallel"), name=name)(*args)


def _float_keys(x):
    b = lax.bitcast_convert_type(x, jnp.int32)
    return jnp.where(b < 0, b ^ jnp.int32(0x7FFFFFFF), b)


def _kth_largest(count_ge, shape, k):
    def body(it, ans):
        cand = ans + lax.shift_left(jnp.int32(1), 31 - it)
        return jnp.where(count_ge(cand) >= k, cand, ans)
    return lax.fori_loop(0, 32, body, jnp.full(shape, INT_MIN, jnp.int32))


def _topk_lanes(gate, k):
    lane = lax.broadcasted_iota(jnp.int32, gate.shape, 1)
    sel = jnp.zeros(gate.shape, jnp.float32)
    g = gate
    for _ in range(k):
        m = jnp.max(g, axis=-1, keepdims=True)
        idx = jnp.min(jnp.where(g == m, lane, LANE), axis=-1, keepdims=True)
        hit = lane == idx
        sel = jnp.where(hit, 1.0, sel)
        g = jnp.where(hit, -jnp.inf, g)
    return sel


def _attend(q_ref, k_ref, v_ref, bias_ref, o_ref, rep, scale, width):
    k = k_ref[:width, :].astype(MXU_DTYPE)
    v = v_ref[:width, :].astype(MXU_DTYPE)
    bias = bias_ref[:, :width]
    for r in range(rep):
        s = _dot_nt(q_ref[r], k) * scale + bias
        m = jnp.max(s, axis=-1, keepdims=True)
        p = jnp.exp(s - m)
        l = jnp.sum(p, axis=-1, keepdims=True)
        o = jnp.dot(p.astype(MXU_DTYPE), v, preferred_element_type=jnp.float32) / l
        o_ref[:, r * HEAD_DIM:(r + 1) * HEAD_DIM] = o.astype(o_ref.dtype)


def _causal_extents(n_tiles, max_branches=4):
    nbr = min(max_branches, n_tiles)
    out, lo = [], 0
    for hi in sorted({-(-n_tiles * (b + 1) // nbr) for b in range(nbr)}):
        out.append((lo, hi))
        lo = hi
    return out


def _dsa_select(qi_ref, wq_ref, kiw_ref, bias_ref, key_ref, i, tq, W, n_keep):
    klane = lax.broadcasted_iota(jnp.int32, (W, LANE), 1)
    ka32 = jnp.where(klane < IDX_DIM, kiw_ref[:W, :], 0.0)
    ka = ka32.astype(MXU_DTYPE)
    kb = pltpu.roll(ka32, IDX_DIM, 1).astype(MXU_DTYPE)
    wq = wq_ref[...]
    wlane = lax.broadcasted_iota(jnp.int32, (tq, LANE), 1)

    def head_weight(h):
        return jnp.sum(jnp.where(wlane == IDX_DIM + h, wq, 0.0), axis=-1, keepdims=True)

    def pair(hp, carry):
        qp = qi_ref[hp]
        c = (jnp.maximum(_dot_nt(qp, ka), 0.0) * head_weight(2 * hp)
             + jnp.maximum(_dot_nt(qp, kb), 0.0) * head_weight(2 * hp + 1))

        bias_ref[:, :W] += c
        return carry
    bias_ref[:, :W] = jnp.zeros((tq, W), jnp.float32)
    lax.fori_loop(0, IDX_HEADS // 2, pair, 0)
    score = bias_ref[:, :W]
    row = lax.broadcasted_iota(jnp.int32, (tq, W), 0) + i * tq
    col = lax.broadcasted_iota(jnp.int32, (tq, W), 1)
    causal = col <= row
    key_ref[:, :W] = jnp.where(causal, _float_keys(score), INT_MIN)

    def count_ge(cand):
        return jnp.sum(jnp.where(key_ref[:, :W] >= cand, 1, 0), axis=-1, keepdims=True)

    thr = _kth_largest(count_ge, (tq, 1), n_keep)
    keys = key_ref[:, :W]
    gt = keys > thr
    eq = keys == thr
    n_gt = jnp.sum(jnp.where(gt, 1, 0), axis=-1, keepdims=True)
    n_eq = jnp.sum(jnp.where(eq, 1, 0), axis=-1, keepdims=True)
    bias_ref[:, :W] = jnp.where((gt | eq) & causal, 0.0, NEG_BIAS)
    tie = (thr > INT_MIN) & (n_gt + n_eq > n_keep)

    @pl.when(jnp.max(jnp.where(tie, 1, 0)) > 0)
    def _():
        need = n_keep - n_gt
        nbits = W.bit_length()

        def body(it, lim):
            cand = lim + lax.shift_left(jnp.int32(1), nbits - 1 - it)
            c = jnp.sum(jnp.where(eq & (col < cand), 1, 0), axis=-1, keepdims=True)
            return jnp.where(c <= need, cand, lim)

        lim = lax.fori_loop(0, nbits, body, jnp.zeros((tq, 1), jnp.int32))
        bias_ref[:, :W] = jnp.where((gt | (eq & (col < lim))) & causal, 0.0, NEG_BIAS)


def _dsa_prompt_kernel(q_ref, k_ref, v_ref, qi_ref, wq_ref, kiw_ref, o_ref, bias_ref, key_ref,
                       *, tq, T, rep, n_keep, scale):
    i = pl.program_id(1)
    g = pl.program_id(2)
    for lo, hi in _causal_extents(T // tq):
        @pl.when((i >= lo) & (i < hi))
        def _(W=hi * tq):
            @pl.when(g == 0)
            def _():
                _dsa_select(qi_ref, wq_ref, kiw_ref, bias_ref, key_ref, i, tq, W, n_keep)

            _attend(q_ref, k_ref, v_ref, bias_ref, o_ref, rep, scale, W)


def _dsa_prompt(q128, k128, v128, qi, kiwi, B, T, rep):
    tq = min(256, T)
    assert T % tq == 0 and tq % LANE == 0
    nT = T // tq
    n_keep = min(DSA_TOPK, T // 4)
    kern = functools.partial(_dsa_prompt_kernel, tq=tq, T=T, rep=rep, n_keep=n_keep,
                             scale=HEAD_DIM ** -0.5)
    return pl.pallas_call(
        kern, grid=(B, nT, KV_A),
        in_specs=[
            pl.BlockSpec((rep, tq, HEAD_DIM), lambda b, i, g: (g, b * nT + i, 0)),
            pl.BlockSpec((T, HEAD_DIM), lambda b, i, g: (b, g)),
            pl.BlockSpec((T, HEAD_DIM), lambda b, i, g: (b, g)),
            pl.BlockSpec((IDX_HEADS // 2, tq, LANE), lambda b, i, g: (0, b * nT + i, 0)),
            pl.BlockSpec((tq, LANE), lambda b, i, g: (b * nT + i, 0)),
            pl.BlockSpec((T, LANE), lambda b, i, g: (b, 0)),
        ],
        out_specs=pl.BlockSpec((tq, rep * HEAD_DIM), lambda b, i, g: (b * nT + i, g)),
        out_shape=jax.ShapeDtypeStruct((B * T, KV_A * rep * HEAD_DIM), MXU_DTYPE),
        scratch_shapes=[pltpu.VMEM((tq, T), jnp.float32), pltpu.VMEM((tq, T), jnp.int32)],
        compiler_params=_cparams("parallel", "parallel", "arbitrary"), name="dsa_prompt",
    )(q128, k128, v128, qi, kiwi, kiwi)


def _moba_prompt_kernel(q_ref, k_ref, v_ref, o_ref, bias_ref, *, bs, nblk, rep, n_pick, scale):
    i = pl.program_id(1)
    qsum = q_ref[0].astype(jnp.float32)
    for r in range(1, rep):
        qsum = qsum + q_ref[r].astype(jnp.float32)
    lane = lax.broadcasted_iota(jnp.int32, (bs, LANE), 1)
    past = lane < i
    row = lax.broadcasted_iota(jnp.int32, (bs, bs), 0)
    col = lax.broadcasted_iota(jnp.int32, (bs, bs), 1)
    own_bias = jnp.where(col <= row, 0.0, NEG_BIAS)
    for lo, hi in _causal_extents(nblk):
        @pl.when((i >= lo) & (i < hi))
        def _(nb=hi):
            gate = jnp.full((bs, LANE), -jnp.inf, jnp.float32)
            for n in range(nb):
                mean_n = jnp.mean(k_ref[n * bs:(n + 1) * bs, :], axis=0, keepdims=True)
                gate = jnp.where(lane == n, jnp.sum(qsum * mean_n, axis=-1, keepdims=True), gate)
            sel = jnp.where(past, _topk_lanes(jnp.where(past, gate, -jnp.inf), n_pick), 0.0)
            for n in range(nb):
                picked = jnp.where(sel[:, n:n + 1] > 0.5, 0.0, NEG_BIAS)
                bias_ref[:, n * bs:(n + 1) * bs] = jnp.where(i == n, own_bias, jnp.broadcast_to(picked, (bs, bs)))
            _attend(q_ref, k_ref, v_ref, bias_ref, o_ref, rep, scale, nb * bs)


def _moba_prompt(q128, k128, v128, B, T, rep, h_a):
    bs = MOBA_BLOCK
    assert T % bs == 0
    nblk = T // bs
    assert 1 <= nblk <= LANE
    kern = functools.partial(_moba_prompt_kernel, bs=bs, nblk=nblk, rep=rep,
                             n_pick=min(MOBA_TOPK, nblk), scale=HEAD_DIM ** -0.5)
    return pl.pallas_call(
        kern, grid=(B, nblk, KV_B),
        in_specs=[
            pl.BlockSpec((rep, bs, HEAD_DIM), lambda b, i, g: (h_a // rep + g, b * nblk + i, 0)),
            pl.BlockSpec((T, HEAD_DIM), lambda b, i, g: (b, KV_A + g)),
            pl.BlockSpec((T, HEAD_DIM), lambda b, i, g: (b, KV_A + g)),
        ],
        out_specs=pl.BlockSpec((bs, rep * HEAD_DIM), lambda b, i, g: (b * nblk + i, g)),
        out_shape=jax.ShapeDtypeStruct((B * T, KV_B * rep * HEAD_DIM), MXU_DTYPE),
        scratch_shapes=[pltpu.VMEM((bs, T), jnp.float32)],
        compiler_params=_cparams("parallel", "parallel", "parallel"), name="moba_prompt",
    )(q128, k128, v128)


def _page_copies(pt_ref, pool_ref, layer, buf, sem, b, chunk, slot, pages, page, groups=0):
    out = []
    for p in range(pages):
        phys = pt_ref[b, chunk * pages + p]
        rows = pl.ds(p * page, page)
        if not groups:
            out.append(pltpu.make_async_copy(pool_ref.at[layer, phys], buf.at[slot, rows], sem.at[slot]))
        for g in range(groups):
            out.append(pltpu.make_async_copy(pool_ref.at[layer, phys, :, g, :], buf.at[slot, g, rows], sem.at[slot]))
    return out


def _paged_step(c, nchunk, start_fn, wait_fn):
    @pl.when(c == 0)
    def _():
        start_fn(0, 0)

    slot = c % 2
    wait_fn(c, slot)

    @pl.when(c + 1 < nchunk)
    def _():
        start_fn(c + 1, 1 - slot)

    return slot


def _dsa_decode_mask_kernel(pt_ref, qi_ref, wc_ref, pool_ref, knew_ref, bp_ref, bn_ref,
                            kbuf, sem, sp_ref, sn_ref, *, layer, nchunk, pages, page, Td, n_keep, past):
    b = pl.program_id(0)
    c = pl.program_id(1)
    CH = pages * page

    def start_fn(cc, slot):
        for cp in _page_copies(pt_ref, pool_ref, layer, kbuf, sem, b, cc, slot, pages, page):
            cp.start()

    def wait_fn(cc, slot):
        for cp in _page_copies(pt_ref, pool_ref, layer, kbuf, sem, b, cc, slot, pages, page):
            cp.wait()

    def scores(kc, width):
        d = _dot_nt(qi_ref[0], kc.astype(MXU_DTYPE))
        r = jnp.maximum(d, 0.0) * wc_ref[0]
        return jnp.sum(r.reshape(IDX_HEADS, Td, width), axis=0)

    @pl.when(c < nchunk)
    def _():
        slot = _paged_step(c, nchunk, start_fn, wait_fn)
        sp_ref[c] = scores(kbuf[slot], CH)

    @pl.when(c == nchunk)
    def _():
        sn_ref[...] = scores(knew_ref[0], LANE)
        trow = lax.broadcasted_iota(jnp.int32, (Td, LANE), 0)
        ncol = lax.broadcasted_iota(jnp.int32, (Td, LANE), 1)
        new_ok = ncol <= trow
        kp = _float_keys(sp_ref[...])
        kn = jnp.where(new_ok, _float_keys(sn_ref[...]), INT_MIN)

        def count(mp, mn):
            return (jnp.sum(jnp.sum(jnp.where(mp, 1, 0), axis=0), axis=-1, keepdims=True)
                    + jnp.sum(jnp.where(mn, 1, 0), axis=-1, keepdims=True))

        thr = _kth_largest(lambda cand: count(kp >= cand[None], kn >= cand), (Td, 1), n_keep)
        gt_p, eq_p = kp > thr[None], kp == thr[None]
        gt_n, eq_n = kn > thr, kn == thr
        n_gt = count(gt_p, gt_n)
        n_eq = count(eq_p, eq_n)
        bp_ref[0] = jnp.where(gt_p | eq_p, 0.0, NEG_BIAS)
        bn_ref[0] = jnp.where((gt_n | eq_n) & new_ok, 0.0, NEG_BIAS)
        tie = (thr > INT_MIN) & (n_gt + n_eq > n_keep)

        @pl.when(jnp.max(jnp.where(tie, 1, 0)) > 0)
        def _():
            need = n_keep - n_gt
            pos_p = (lax.broadcasted_iota(jnp.int32, (nchunk, Td, CH), 0) * CH
                     + lax.broadcasted_iota(jnp.int32, (nchunk, Td, CH), 2))
            pos_n = past + ncol
            nbits = (past + LANE).bit_length()

            def body(it, lim):
                cand = lim + lax.shift_left(jnp.int32(1), nbits - 1 - it)
                cnt = count(eq_p & (pos_p < cand[None]), eq_n & (pos_n < cand))
                return jnp.where(cnt <= need, cand, lim)

            lim = lax.fori_loop(0, nbits, body, jnp.zeros((Td, 1), jnp.int32))
            bp_ref[0] = jnp.where(gt_p | (eq_p & (pos_p < lim[None])), 0.0, NEG_BIAS)
            bn_ref[0] = jnp.where((gt_n | (eq_n & (pos_n < lim))) & new_ok, 0.0, NEG_BIAS)


def _dsa_decode_mask(page_table, qi_s, wcol, pool, knew, layer, Td):
    Bd, n_pages = page_table.shape
    page = pool.shape[2]
    pages = min(PAGES_PER_CHUNK, n_pages)
    assert n_pages % pages == 0
    nchunk = n_pages // pages
    CH = pages * page
    past = n_pages * page
    n_keep = min(DSA_TOPK, (past + Td) // 4)
    HT = qi_s.shape[1]
    kern = functools.partial(_dsa_decode_mask_kernel, layer=layer, nchunk=nchunk, pages=pages, page=page,
                             Td=Td, n_keep=n_keep, past=past)
    grid_spec = pltpu.PrefetchScalarGridSpec(
        num_scalar_prefetch=1, grid=(Bd, nchunk + 1),
        in_specs=[
            pl.BlockSpec((1, HT, IDX_DIM), lambda b, c, pt: (b, 0, 0)),
            pl.BlockSpec((1, HT, 1), lambda b, c, pt: (b, 0, 0)),
            pl.BlockSpec(memory_space=pl.ANY),
            pl.BlockSpec((1, LANE, IDX_DIM), lambda b, c, pt: (b, 0, 0)),
        ],
        out_specs=[
            pl.BlockSpec((1, nchunk, Td, CH), lambda b, c, pt: (b, 0, 0, 0)),
            pl.BlockSpec((1, Td, LANE), lambda b, c, pt: (b, 0, 0)),
        ],
        scratch_shapes=[
            pltpu.VMEM((2, CH, IDX_DIM), jnp.float32),
            pltpu.SemaphoreType.DMA((2,)),
            pltpu.VMEM((nchunk, Td, CH), jnp.float32),
            pltpu.VMEM((Td, LANE), jnp.float32),
        ])
    return pl.pallas_call(
        kern, grid_spec=grid_spec,
        out_shape=[jax.ShapeDtypeStruct((Bd, nchunk, Td, CH), jnp.float32),
                   jax.ShapeDtypeStruct((Bd, Td, LANE), jnp.float32)],
        compiler_params=_cparams("arbitrary", "arbitrary"), name="dsa_decode_mask",
    )(page_table, qi_s, wcol, pool, knew)


def _moba_decode_mask_kernel(pt_ref, q_ref, pool_ref, bp_ref, kbuf, sem, gate_ref,
                             *, layer, nchunk, pages, page, Td, rep, bs, n_pick):
    b = pl.program_id(0)
    c = pl.program_id(1)
    CH = pages * page
    per = CH // bs
    nblk = nchunk * per

    def start_fn(cc, slot):
        for cp in _page_copies(pt_ref, pool_ref, layer, kbuf, sem, b, cc, slot, pages, page, KV_B):
            cp.start()

    def wait_fn(cc, slot):
        for cp in _page_copies(pt_ref, pool_ref, layer, kbuf, sem, b, cc, slot, pages, page, KV_B):
            cp.wait()

    @pl.when(c == 0)
    def _():
        gate_ref[...] = jnp.full(gate_ref.shape, -jnp.inf, jnp.float32)

    slot = _paged_step(c, nchunk, start_fn, wait_fn)
    lane = lax.broadcasted_iota(jnp.int32, (Td, LANE), 1)
    for g in range(KV_B):
        qsum = jnp.sum(q_ref[0, g].astype(jnp.float32).reshape(rep, Td, HEAD_DIM), axis=0)
        gate = gate_ref[g]
        for nb in range(per):
            mean_nb = jnp.mean(kbuf[slot, g, nb * bs:(nb + 1) * bs, :], axis=0, keepdims=True)
            gate = jnp.where(lane == c * per + nb, jnp.sum(qsum * mean_nb, axis=-1, keepdims=True), gate)
        gate_ref[g] = gate

    @pl.when(c == nchunk - 1)
    def _():
        for g in range(KV_B):
            sel = jnp.where(lane < nblk, _topk_lanes(gate_ref[g], n_pick), 0.0)
            for n in range(nblk):
                picked = jnp.where(sel[:, n:n + 1] > 0.5, 0.0, NEG_BIAS)
                bp_ref[0, n // per, g, :, (n % per) * bs:(n % per + 1) * bs] = jnp.broadcast_to(picked, (Td, bs))


def _moba_decode_mask(page_table, q_s, pool, layer, Td, rep):
    Bd, n_pages = page_table.shape
    page = pool.shape[2]
    pages = min(PAGES_PER_CHUNK, n_pages)
    assert n_pages % pages == 0
    nchunk = n_pages // pages
    CH = pages * page
    bs = MOBA_BLOCK
    assert CH % bs == 0 and Td <= bs
    nblk = n_pages * page // bs
    assert nblk <= LANE
    kern = functools.partial(_moba_decode_mask_kernel, layer=layer, nchunk=nchunk, pages=pages, page=page,
                             Td=Td, rep=rep, bs=bs, n_pick=min(MOBA_TOPK, (n_pages * page + Td) // bs))
    grid_spec = pltpu.PrefetchScalarGridSpec(
        num_scalar_prefetch=1, grid=(Bd, nchunk),
        in_specs=[
            pl.BlockSpec((1, KV_B, rep * Td, HEAD_DIM), lambda b, c, pt: (b, 0, 0, 0)),
            pl.BlockSpec(memory_space=pl.ANY),
        ],
        out_specs=pl.BlockSpec((1, nchunk, KV_B, Td, CH), lambda b, c, pt: (b, 0, 0, 0, 0)),
        scratch_shapes=[
            pltpu.VMEM((2, KV_B, CH, HEAD_DIM), jnp.float32),
            pltpu.SemaphoreType.DMA((2,)),
            pltpu.VMEM((KV_B, Td, LANE), jnp.float32),
        ])
    return pl.pallas_call(
        kern, grid_spec=grid_spec,
        out_shape=jax.ShapeDtypeStruct((Bd, nchunk, KV_B, Td, CH), jnp.float32),
        compiler_params=_cparams("arbitrary", "arbitrary"), name="moba_decode_mask",
    )(page_table, q_s, pool)


def _paged_attn_kernel(pt_ref, q_ref, kpool_ref, vpool_ref, knew_ref, vnew_ref, bp_ref, bn_ref, o_ref,
                       kbuf, vbuf, sem, m_ref, l_ref, acc_ref,
                       *, layer, nchunk, pages, page, Td, rep, n_kv, per_group_bias, scale):
    b = pl.program_id(0)
    c = pl.program_id(1)

    def start_fn(cc, slot):
        for cp in (_page_copies(pt_ref, kpool_ref, layer, kbuf, sem.at[0], b, cc, slot, pages, page, n_kv)
                   + _page_copies(pt_ref, vpool_ref, layer, vbuf, sem.at[1], b, cc, slot, pages, page, n_kv)):
            cp.start()

    def wait_fn(cc, slot):
        for cp in (_page_copies(pt_ref, kpool_ref, layer, kbuf, sem.at[0], b, cc, slot, pages, page, n_kv)
                   + _page_copies(pt_ref, vpool_ref, layer, vbuf, sem.at[1], b, cc, slot, pages, page, n_kv)):
            cp.wait()

    @pl.when(c == 0)
    def _():
        m_ref[...] = jnp.full(m_ref.shape, -jnp.inf, jnp.float32)
        l_ref[...] = jnp.zeros(l_ref.shape, jnp.float32)
        acc_ref[...] = jnp.zeros(acc_ref.shape, jnp.float32)

    def process(k_fn, v_fn, bias_fn):
        for g in range(n_kv):
            k = k_fn(g).astype(MXU_DTYPE)
            v = v_fn(g).astype(MXU_DTYPE)
            bias = bias_fn(g)
            s = _dot_nt(q_ref[0, g], k) * scale + jnp.concatenate([bias] * rep, axis=0)
            m_old = m_ref[g]
            m_new = jnp.maximum(m_old, jnp.max(s, axis=-1, keepdims=True))
            a = jnp.exp(m_old - m_new)
            p = jnp.exp(s - m_new)
            l_ref[g] = a * l_ref[g] + jnp.sum(p, axis=-1, keepdims=True)
            acc_ref[g] = a * acc_ref[g] + jnp.dot(p.astype(MXU_DTYPE), v, preferred_element_type=jnp.float32)
            m_ref[g] = m_new

    @pl.when(c < nchunk)
    def _():
        slot = _paged_step(c, nchunk, start_fn, wait_fn)
        process(lambda g: kbuf[slot, g], lambda g: vbuf[slot, g],
                lambda g: bp_ref[0, 0, g if per_group_bias else 0])

    @pl.when(c == nchunk)
    def _():
        process(lambda g: knew_ref[0, :, g * HEAD_DIM:(g + 1) * HEAD_DIM],
                lambda g: vnew_ref[0, :, g * HEAD_DIM:(g + 1) * HEAD_DIM], lambda g: bn_ref[0])
        for g in range(n_kv):
            o_ref[0, g] = acc_ref[g] / l_ref[g]


def _paged_attention(page_table, q_s, kpool, vpool, knew, vnew, bias_past, bias_new, layer, Td, rep):
    Bd, n_pages = page_table.shape
    page = kpool.shape[2]
    pages = min(PAGES_PER_CHUNK, n_pages)
    nchunk = n_pages // pages
    CH = pages * page
    n_kv = q_s.shape[1]
    R = rep * Td
    W = n_kv * HEAD_DIM
    gb = bias_past.shape[2]
    nb_new = bias_new.shape[0]
    kern = functools.partial(_paged_attn_kernel, layer=layer, nchunk=nchunk, pages=pages, page=page, Td=Td,
                             rep=rep, n_kv=n_kv, per_group_bias=gb > 1, scale=HEAD_DIM ** -0.5)
    grid_spec = pltpu.PrefetchScalarGridSpec(
        num_scalar_prefetch=1, grid=(Bd, nchunk + 1),
        in_specs=[
            pl.BlockSpec((1, n_kv, R, HEAD_DIM), lambda b, c, pt: (b, 0, 0, 0)),
            pl.BlockSpec(memory_space=pl.ANY),
            pl.BlockSpec(memory_space=pl.ANY),
            pl.BlockSpec((1, LANE, W), lambda b, c, pt: (b, 0, 0)),
            pl.BlockSpec((1, LANE, W), lambda b, c, pt: (b, 0, 0)),
            pl.BlockSpec((1, 1, gb, Td, CH), lambda b, c, pt: (b, jnp.minimum(c, nchunk - 1), 0, 0, 0)),
            pl.BlockSpec((1, Td, LANE), lambda b, c, pt: (b if nb_new > 1 else 0, 0, 0)),
        ],
        out_specs=pl.BlockSpec((1, n_kv, R, HEAD_DIM), lambda b, c, pt: (b, 0, 0, 0)),
        scratch_shapes=[
            pltpu.VMEM((2, n_kv, CH, HEAD_DIM), jnp.float32),
            pltpu.VMEM((2, n_kv, CH, HEAD_DIM), jnp.float32),
            pltpu.SemaphoreType.DMA((2, 2)),
            pltpu.VMEM((n_kv, R, 1), jnp.float32),
            pltpu.VMEM((n_kv, R, 1), jnp.float32),
            pltpu.VMEM((n_kv, R, HEAD_DIM), jnp.float32),
        ])
    return pl.pallas_call(
        kern, grid_spec=grid_spec,
        out_shape=jax.ShapeDtypeStruct((Bd, n_kv, R, HEAD_DIM), jnp.float32),
        compiler_params=_cparams("arbitrary", "arbitrary"), name="paged_attention",
    )(page_table, q_s, kpool, vpool, knew, vnew, bias_past, bias_new)


def _route_kernel(lg_ref, eid_ref, gate_ref):
    lg = lg_ref[...]
    lane = lax.broadcasted_iota(jnp.int32, lg.shape, 1)
    gmask = lane < N_GROUPS
    gl = jnp.where(gmask, lg, -jnp.inf)
    gmax = jnp.max(gl, axis=-1, keepdims=True)
    gsel = jnp.min(jnp.where(gl == gmax, lane, LANE), axis=-1, keepdims=True)
    g_w = 1.0 / jnp.sum(jnp.where(gmask, jnp.exp(gl - gmax), 0.0), axis=-1, keepdims=True)
    lo = N_GROUPS + gsel * EXPERTS_PER_GROUP
    el = jnp.where((lane >= lo) & (lane < lo + EXPERTS_PER_GROUP), lg, -jnp.inf)
    v1 = jnp.max(el, axis=-1, keepdims=True)
    i1 = jnp.min(jnp.where(el == v1, lane, LANE), axis=-1, keepdims=True)
    el2 = jnp.where(lane == i1, -jnp.inf, el)
    v2 = jnp.max(el2, axis=-1, keepdims=True)
    i2 = jnp.min(jnp.where(el2 == v2, lane, LANE), axis=-1, keepdims=True)
    e2 = jnp.exp(v2 - v1)
    p1 = 1.0 / (1.0 + e2)
    eid_ref[...] = jnp.where(lane == 0, i1 - N_GROUPS, jnp.where(lane == 1, i2 - N_GROUPS, 0))
    gate_ref[...] = jnp.where(lane == 0, g_w * p1, jnp.where(lane == 1, g_w * (e2 * p1), 0.0))


def _route(logits):
    Np = logits.shape[0]
    tr = _pick_tile(Np, 1024, 8)
    spec = pl.BlockSpec((tr, LANE), lambda i: (i, 0))
    return pl.pallas_call(
        _route_kernel, grid=(Np // tr,), in_specs=[spec], out_specs=[spec, spec],
        out_shape=[jax.ShapeDtypeStruct((Np, LANE), jnp.int32), jax.ShapeDtypeStruct((Np, LANE), jnp.float32)],
        compiler_params=_cparams("parallel"), name="route")(logits)


def _moe_plan(eid, n_experts, rc, nch_max):
    M = eid.shape[0]
    C = rc * nch_max
    order = jnp.argsort(eid, stable=True).astype(jnp.int32)
    eid_s = eid[order]
    counts = jnp.bincount(eid, length=n_experts).astype(jnp.int32)
    start = jnp.cumsum(counts) - counts
    pcounts = (counts + rc - 1) // rc * rc
    pstart = jnp.cumsum(pcounts) - pcounts
    slot_s = (pstart[eid_s] + jnp.arange(M, dtype=jnp.int32) - start[eid_s]).astype(jnp.int32)
    R = _round_up(M, rc) + n_experts * rc
    pend = pstart + pcounts
    slots = jnp.arange(R, dtype=jnp.int32)
    e_slot = jnp.minimum(jnp.sum(pend[None, :] <= slots[:, None], axis=1), n_experts - 1)
    j_slot = slots - pstart[e_slot]
    src = jnp.clip(start[e_slot] + j_slot, 0, M - 1)
    tok = jnp.where((j_slot < counts[e_slot]) & (slots < pend[-1]), order[src] // EXPERT_TOPK, 0)
    dest = slot_s[jnp.argsort(order)]
    nseg = (pcounts + C - 1) // C
    send = jnp.cumsum(nseg)
    sstart = send - nseg
    n_seg = -(-R // C) + n_experts
    sidx = jnp.arange(n_seg, dtype=jnp.int32)
    e_of = jnp.minimum(jnp.sum(send[None, :] <= sidx[:, None], axis=1), n_experts - 1).astype(jnp.int32)
    active = sidx < send[-1]
    local = sidx - sstart[e_of]
    nch = jnp.where(active, jnp.clip((pcounts[e_of] - local * C + rc - 1) // rc, 0, nch_max), 0)
    used = pstart[-1] + pcounts[-1]
    idle_row0 = used + (sidx - send[-1]) * C
    row0 = jnp.where(active, pstart[e_of] + local * C, jnp.minimum(idle_row0, R))
    nzero = jnp.where(active, 0, jnp.clip((R - idle_row0) // rc, 0, nch_max))
    seg_e = jnp.where(active, e_of, e_of[jnp.maximum(send[-1] - 1, 0)])
    i32 = jnp.int32
    return tok, dest, seg_e.astype(i32), row0.astype(i32), nch.astype(i32), nzero.astype(i32), R


def _moe_kernel(seg_e_ref, row0_ref, nch_ref, nzero_ref, tok_ref, x_hbm, wg_ref, wu_ref, wd_ref, y_hbm,
                xp, a_acc, u_acc, hbuf, wgb, wub, wdb, ostage, gsem, osem, *, rc, nk, nn, n_seg):
    del seg_e_ref
    s = pl.program_id(0)
    j = pl.program_id(1)
    nch = nch_ref[s]
    row0 = row0_ref[s]
    nzero = nzero_ref[s]
    cur = s % 2
    kt = wgb.shape[0]

    def chunk_rows(ch):
        return pl.ds(pl.multiple_of(ch * rc, rc), rc)

    def out_copy(slot, ch, n):
        dst = y_hbm.at[pl.ds(pl.multiple_of(row0 + ch * rc, rc), rc), n, :]
        return pltpu.make_async_copy(ostage.at[slot], dst, osem.at[slot])

    @pl.when((j == 0) & (nzero > 0))
    def _():
        ostage[0] = jnp.zeros(ostage.shape[1:], jnp.float32)

        def start(ch, carry):
            for n in range(nn):
                out_copy(0, ch, n).start()
            return carry

        def wait(ch, carry):
            for n in range(nn):
                out_copy(0, ch, n).wait()
            return carry
        lax.fori_loop(0, nzero, start, 0)
        lax.fori_loop(0, nzero, wait, 0)

    def gather(seg, buf, wait):
        base = row0_ref[seg]

        def body(r8, carry):
            for u in range(8):
                r = r8 * 8 + u
                cp = pltpu.make_async_copy(x_hbm.at[pl.ds(tok_ref[base + r], 1)], xp.at[buf, pl.ds(r, 1)],
                                           gsem.at[buf])
                if wait:
                    cp.wait()
                else:
                    cp.start(priority=GATHER_DMA_PRIORITY)
            return carry
        lax.fori_loop(0, nch_ref[seg] * (rc // 8), body, 0)

    @pl.when((j == 0) & (nch > 0))
    def _():
        @pl.when(s == 0)
        def _():
            gather(0, 0, False)

        gather(s, cur, True)
        nxt = jnp.minimum(s + 1, n_seg - 1)

        @pl.when((s + 1 < n_seg) & (nch_ref[nxt] > 0))
        def _():
            gather(nxt, 1 - cur, False)

    @pl.when((j < nk) & (nch > 0))
    def _():
        wgb[...] = wg_ref[...].astype(wgb.dtype)
        wub[...] = wu_ref[...].astype(wub.dtype)

    for jj in range(nk):
        @pl.when((j == jj) & (nch > 0))
        def _(jj=jj):
            def body(ch, carry):
                rows = chunk_rows(ch)
                w = xp[cur, rows, (jj // 2) * kt:(jj // 2 + 1) * kt]
                if jj % 2 == 0:
                    xf = lax.bitcast_convert_type(lax.shift_left(w, jnp.uint32(16)), jnp.float32)
                else:
                    xf = lax.bitcast_convert_type(w & jnp.uint32(0xFFFF0000), jnp.float32)
                x = xf.astype(wgb.dtype)
                a = jnp.dot(x, wgb[...], preferred_element_type=jnp.float32)
                u = jnp.dot(x, wub[...], preferred_element_type=jnp.float32)
                if jj == 0:
                    a_acc[rows, :] = a
                    u_acc[rows, :] = u
                else:
                    a_acc[rows, :] += a
                    u_acc[rows, :] += u
                return carry
            lax.fori_loop(0, nch, body, 0)

    @pl.when((j == nk - 1) & (nch > 0))
    def _():
        def body(ch, carry):
            rows = chunk_rows(ch)
            a = a_acc[rows, :]
            hbuf[rows, :] = (a * jax.nn.sigmoid(a) * u_acc[rows, :]).astype(hbuf.dtype)
            return carry
        lax.fori_loop(0, nch, body, 0)

    @pl.when((j >= nk) & (nch > 0))
    def _():
        wdb[...] = wd_ref[...].astype(wdb.dtype)

    for n in range(nn):
        @pl.when((j == nk + n) & (nch > 0))
        def _(n=n):
            def drain(m):
                def body(ch, carry):
                    out_copy(ch, ch, m).wait()
                    return carry
                lax.fori_loop(0, nch, body, 0)

            if n > 0:
                drain(n - 1)

            def body(ch, carry):
                ostage[ch] = jnp.dot(hbuf[chunk_rows(ch), :], wdb[...], preferred_element_type=jnp.float32)
                out_copy(ch, ch, n).start(priority=GATHER_DMA_PRIORITY)
                return carry
            lax.fori_loop(0, nch, body, 0)

            if n == nn - 1:
                drain(n)


def _moe_experts(xpk, kt, plan, w_gate, w_up, w_down, layer):
    tok, _, seg_e, row0, nch, nzero, R = plan
    D = 2 * xpk.shape[1]
    nk = D // kt
    F = w_gate.shape[-1]
    nt = min(MOE_NT, D)
    nn = D // nt
    rc = MOE_ROWS
    C = rc * MOE_CHUNKS
    n_seg = seg_e.shape[0]

    def k_idx(s, j, nc):
        return jnp.where(nc[s] > 0, jnp.minimum(j, nk - 1), nk - 1)

    def n_idx(s, j, nc):
        return jnp.where(nc[s] > 0, jnp.maximum(j - nk, 0), nn - 1)

    grid_spec = pltpu.PrefetchScalarGridSpec(
        num_scalar_prefetch=5, grid=(n_seg, nk + nn),
        in_specs=[
            pl.BlockSpec(memory_space=pl.ANY),
            pl.BlockSpec((None, None, kt, F), lambda s, j, se, r0, nc, nz, tk: (layer, se[s], k_idx(s, j, nc), 0)),
            pl.BlockSpec((None, None, kt, F), lambda s, j, se, r0, nc, nz, tk: (layer, se[s], k_idx(s, j, nc), 0)),
            pl.BlockSpec((None, None, F, nt), lambda s, j, se, r0, nc, nz, tk: (layer, se[s], 0, n_idx(s, j, nc))),
        ],
        out_specs=pl.BlockSpec(memory_space=pl.ANY),
        scratch_shapes=[
            pltpu.VMEM((2, C, D // 2), jnp.uint32),
            pltpu.VMEM((C, F), jnp.float32),
            pltpu.VMEM((C, F), jnp.float32),
            pltpu.VMEM((C, F), MXU_DTYPE),
            pltpu.VMEM((kt, F), MXU_DTYPE),
            pltpu.VMEM((kt, F), MXU_DTYPE),
            pltpu.VMEM((F, nt), MXU_DTYPE),
            pltpu.VMEM((MOE_CHUNKS, rc, nt), jnp.float32),
            pltpu.SemaphoreType.DMA((2,)),
            pltpu.SemaphoreType.DMA((MOE_CHUNKS,)),
        ])
    return pl.pallas_call(
        functools.partial(_moe_kernel, rc=rc, nk=nk, nn=nn, n_seg=n_seg), grid_spec=grid_spec,
        out_shape=jax.ShapeDtypeStruct((R, nn, nt), jnp.float32),
        compiler_params=_cparams("arbitrary", "arbitrary"), name="moe_experts",
    )(seg_e, row0, nch, nzero, tok, xpk, w_gate, w_up, w_down)


def _combine_kernel(dest_ref, x_ref, gate_ref, gam_ref, bet_ref, y_hbm, o32_ref, o16_ref, ybuf, sem,
                    *, tc, alpha, n_tiles):
    i = pl.program_id(0)
    cur = i % 2

    def rows(tile, buf, wait):
        def body(r, carry):
            for k in range(EXPERT_TOPK):
                slot = dest_ref[(tile * tc + r) * EXPERT_TOPK + k]
                cp = pltpu.make_async_copy(y_hbm.at[slot], ybuf.at[buf, k, r], sem.at[buf])
                if wait:
                    cp.wait()
                else:
                    cp.start(priority=GATHER_DMA_PRIORITY)
            return carry
        lax.fori_loop(0, tc, body, 0)

    @pl.when(i == 0)
    def _():
        rows(0, 0, False)

    rows(i, cur, True)

    @pl.when(i + 1 < n_tiles)
    def _():
        rows(i + 1, 1 - cur, False)

    gate = gate_ref[...]
    nn, nt = ybuf.shape[3], ybuf.shape[4]
    parts = []
    for n in range(nn):
        ffn = gate[:, 0:1] * ybuf[cur, 0, :, n, :]
        for k in range(1, EXPERT_TOPK):
            ffn = ffn + gate[:, k:k + 1] * ybuf[cur, k, :, n, :]
        parts.append(alpha * x_ref[:, n * nt:(n + 1) * nt] + ffn)
    v = jnp.concatenate(parts, axis=1)
    mu = jnp.mean(v, axis=-1, keepdims=True)
    d = v - mu
    var = jnp.mean(d * d, axis=-1, keepdims=True)
    y = d * lax.rsqrt(var + LN_EPS) * gam_ref[...] + bet_ref[...]
    o32_ref[...] = y
    o16_ref[...] = y.astype(o16_ref.dtype)


def _moe_combine(dest, x32, gates, y_sorted, gam, bet, alpha):
    Np, D = x32.shape
    tc = _pick_tile(Np, 192, 16)
    row = lambda i, d: (i, 0)
    grid_spec = pltpu.PrefetchScalarGridSpec(
        num_scalar_prefetch=1, grid=(Np // tc,),
        in_specs=[
            pl.BlockSpec((tc, D), row),
            pl.BlockSpec((tc, LANE), row),
            pl.BlockSpec((1, D), lambda i, d: (0, 0)),
            pl.BlockSpec((1, D), lambda i, d: (0, 0)),
            pl.BlockSpec(memory_space=pl.ANY),
        ],
        out_specs=[pl.BlockSpec((tc, D), row), pl.BlockSpec((tc, D), row)],
        scratch_shapes=[pltpu.VMEM((2, EXPERT_TOPK, tc) + y_sorted.shape[1:], jnp.float32),
                        pltpu.SemaphoreType.DMA((2,))])
    return pl.pallas_call(
        functools.partial(_combine_kernel, tc=tc, alpha=alpha, n_tiles=Np // tc), grid_spec=grid_spec,
        out_shape=[jax.ShapeDtypeStruct((Np, D), jnp.float32), jax.ShapeDtypeStruct((Np, D), MXU_DTYPE)],
        compiler_params=_cparams("arbitrary"), name="moe_combine",
    )(dest, x32, gates, gam.reshape(1, D), bet.reshape(1, D), y_sorted)


def _rope_tables(pos, scale_wi):
    def cs(dim):
        half = dim // 2
        inv = ROPE_THETA ** (-jnp.arange(half, dtype=jnp.float32) * 2.0 / dim)
        ang = pos.astype(jnp.float32)[:, None] * inv[None, :]
        return jnp.cos(ang), jnp.sin(ang)
    c, s = cs(HEAD_DIM)
    t128 = (jnp.concatenate([c, c], -1), jnp.concatenate([-s, s], -1))
    c, s = cs(IDX_DIM)
    c64 = jnp.concatenate([c, c], -1)
    s64 = jnp.concatenate([-s, s], -1)
    t64 = (jnp.tile(c64, (1, LANE // IDX_DIM)), jnp.tile(s64, (1, LANE // IDX_DIM)))
    n = pos.shape[0]
    pad = LANE - IDX_DIM - IDX_HEADS
    tkw = (jnp.concatenate([c64, jnp.full((n, IDX_HEADS), scale_wi, jnp.float32), jnp.zeros((n, pad), jnp.float32)], -1),
           jnp.concatenate([s64, jnp.zeros((n, LANE - IDX_DIM), jnp.float32)], -1))
    return t128, t64, tkw


def kernel(x_prompt, x_sample, cache_a_k, cache_a_v, cache_a_kidx, cache_b_k, cache_b_v, page_table, p_prompt, p_sample, ln_emb_g, ln_emb_b, w_in, w_out, ln1_g, ln1_b, w_route_group, w_route_expert, w_exp_gate, w_exp_up, w_exp_down, ln2_g, ln2_b, w_ple, w_ple_gate, ln3_g, ln3_b):
    B, T, D = x_prompt.shape
    Bd, Td, _ = x_sample.shape
    depth = w_in.shape[0]
    n_phys, page = cache_a_k.shape[1], cache_a_k.shape[2]
    past = page_table.shape[1] * page
    n_experts = w_exp_gate.shape[1]
    h_a = D // 2 // HEAD_DIM
    h_b = h_a
    rep_a, rep_b = h_a // KV_A, h_b // KV_B
    assert IDX_DIM * 2 == LANE and IDX_DIM + IDX_HEADS <= LANE and page == LANE
    assert N_GROUPS * (1 + EXPERTS_PER_GROUP) <= LANE and n_experts == N_GROUPS * EXPERTS_PER_GROUP
    alpha = (2.0 * depth) ** 0.25
    n_p, n_s = B * T, Bd * Td
    N = n_p + n_s
    Np = _round_up(N, ROW_ALIGN)
    f32 = jnp.float32

    def stream(a_p, a_s):
        w = a_p.shape[-1]
        return jnp.concatenate([a_p.reshape(n_p, w), a_s.reshape(n_s, w), jnp.zeros((Np - N, w), a_p.dtype)], 0)

    pos = jnp.concatenate([jnp.tile(jnp.arange(T, dtype=jnp.int32), B),
                           jnp.tile(past + jnp.arange(Td, dtype=jnp.int32), Bd),
                           jnp.zeros((Np - N,), jnp.int32)])
    t128, t64, tkw = _rope_tables(pos, IDX_HEADS ** -0.5 * IDX_DIM ** -0.5)

    widths = (h_a * HEAD_DIM, KV_A * HEAD_DIM, KV_A * HEAD_DIM, IDX_HEADS * IDX_DIM, IDX_DIM, IDX_HEADS,
              h_b * HEAD_DIM, KV_B * HEAD_DIM, KV_B * HEAD_DIM)
    offs = [0]
    for w in widths:
        offs.append(offs[-1] + w)
    assert offs[-1] == w_in.shape[2]

    def cols(w, *ids):
        return jnp.concatenate([w[:, offs[i]:offs[i + 1]] for i in ids], axis=1).astype(MXU_DTYPE)

    kv_w = KV_A * HEAD_DIM
    trow = jnp.arange(Td, dtype=jnp.int32)[:, None]
    own_bias = jnp.where(jnp.arange(LANE, dtype=jnp.int32)[None, :] <= trow, 0.0, NEG_BIAS).astype(f32)[None]

    def fresh(rows):
        w = rows.shape[-1]
        return jnp.pad(rows.reshape(Bd, Td, w), ((0, 0), (0, LANE - Td), (0, 0)))

    def decode_q(q_sm, kv, rep):
        q = q_sm.reshape(kv, rep, Bd, Td, HEAD_DIM)
        return jnp.transpose(q, (2, 0, 1, 3, 4)).reshape(Bd, kv, rep * Td, HEAD_DIM)

    def decode_o(o, kv, rep):
        o = o.reshape(Bd, kv, rep, Td, HEAD_DIM)
        return jnp.transpose(o, (0, 3, 1, 2, 4)).reshape(n_s, kv * rep * HEAD_DIM)

    x_tail = jnp.concatenate([x_sample.reshape(n_s, D), jnp.zeros((Np - N, D), x_sample.dtype)], axis=0)
    h32, h16 = _layer_norm_embed(x_prompt.reshape(n_p, D), x_tail, ln_emb_g, ln_emb_b)
    rows_out = []
    for l in range(depth):
        wl = w_in[l]
        q128 = _matmul([h16], [cols(wl, 0, 6)], MXU_DTYPE, "rope128", t128, True, name="proj_q")
        k128 = _matmul([h16], [cols(wl, 1, 7)], f32, "rope128", t128, name="proj_k")
        v128 = _matmul([h16], [cols(wl, 2, 8)], f32, name="proj_v")
        qi = _matmul([h16], [cols(wl, 3)], MXU_DTYPE, "rope64", t64, True, name="proj_qi")
        w_kw = jnp.pad(cols(wl, 4, 5), ((0, 0), (0, LANE - IDX_DIM - IDX_HEADS)))
        kiwi = _matmul([h16], [w_kw], f32, "rope64", tkw, name="proj_kiwi")

        attn_a = _dsa_prompt(q128, k128, v128, qi, kiwi, B, T, rep_a)
        attn_b = _moba_prompt(q128, k128, v128, B, T, rep_b, h_a)

        ks, vs = k128[n_p:N], v128[n_p:N]
        kiwi_s = kiwi[n_p:N]
        qi_s = qi[:, n_p:N].reshape(IDX_HEADS // 2, Bd, Td, 2, IDX_DIM)
        qi_s = jnp.transpose(qi_s, (1, 0, 3, 2, 4)).reshape(Bd, IDX_HEADS * Td, IDX_DIM)
        wcol = jnp.transpose(kiwi_s[:, IDX_DIM:IDX_DIM + IDX_HEADS].reshape(Bd, Td, IDX_HEADS), (0, 2, 1))
        wcol = wcol.reshape(Bd, IDX_HEADS * Td, 1)
        bias_p, bias_n = _dsa_decode_mask(page_table, qi_s, wcol, cache_a_kidx, fresh(kiwi_s[:, :IDX_DIM]), l, Td)
        o_a = _paged_attention(page_table, decode_q(q128[:h_a, n_p:N], KV_A, rep_a), cache_a_k, cache_a_v,
                               fresh(ks[:, :kv_w]), fresh(vs[:, :kv_w]), bias_p[:, :, None], bias_n, l, Td, rep_a)
        qb_s = decode_q(q128[h_a:, n_p:N], KV_B, rep_b)
        bias_b = _moba_decode_mask(page_table, qb_s, cache_b_k, l, Td, rep_b)
        o_b = _paged_attention(page_table, qb_s, cache_b_k, cache_b_v, fresh(ks[:, kv_w:]), fresh(vs[:, kv_w:]),
                               bias_b, own_bias, l, Td, rep_b)
        tail = jnp.zeros((Np - N, h_a * HEAD_DIM), MXU_DTYPE)
        attn_a = jnp.concatenate([attn_a, decode_o(o_a, KV_A, rep_a).astype(MXU_DTYPE), tail], axis=0)
        attn_b = jnp.concatenate([attn_b, decode_o(o_b, KV_B, rep_b).astype(MXU_DTYPE), tail], axis=0)

        w_o = w_out[l].astype(MXU_DTYPE)
        mix = _matmul([attn_a, attn_b], [w_o[:h_a * HEAD_DIM], w_o[h_a * HEAD_DIM:]], f32, name="proj_out")
        w_r = jnp.pad(jnp.concatenate([w_route_group[l], w_route_expert[l]], axis=1),
                      ((0, 0), (0, LANE - N_GROUPS - n_experts)))
        w_rh = w_r.astype(MXU_DTYPE)
        w_rl = (w_r - w_rh.astype(f32)).astype(MXU_DTYPE)
        moe_kt = min(MOE_KT, D // 2)
        x32, xpk, logits = _layer_norm([h32, mix], ln1_g[l], ln1_b[l], "add", alpha, (w_rh, w_rl), pack_kt=moe_kt)

        eid, gates = _route(logits)
        plan = _moe_plan(eid[:N, :EXPERT_TOPK].reshape(-1), n_experts, MOE_ROWS, MOE_CHUNKS)
        y_sorted = _moe_experts(xpk, moe_kt, plan, w_exp_gate, w_exp_up, w_exp_down, l)
        dest = jnp.pad(plan[1], (0, (Np - N) * EXPERT_TOPK))
        x32, x16 = _moe_combine(dest, x32, gates, y_sorted, ln2_g[l], ln2_b[l], alpha)

        gate_pre = _matmul([x16], [w_ple_gate[l].astype(MXU_DTYPE)], f32, name="ple_gate")
        p16 = stream(p_prompt[l], p_sample[l]).astype(MXU_DTYPE)
        ple = _matmul([p16], [w_ple[l].astype(MXU_DTYPE)], f32, name="ple_embed")
        h32, h16 = _layer_norm([x32, gate_pre, ple], ln3_g[l], ln3_b[l], "ple", alpha)
        rows_out.append((k128, v128, kiwi))

    def gather_rows(sel, lo, hi, lead):
        return jnp.stack([sel(r)[lo:hi].reshape(lead) for r in rows_out])

    outs = [h32[:n_p].reshape(B, T, D), h32[n_p:N].reshape(Bd, Td, D)]
    for lo, hi, lead in ((0, n_p, (B, T)), (n_p, N, (Bd, Td))):
        outs += [
            gather_rows(lambda r: r[0][:, :kv_w], lo, hi, lead + (KV_A, HEAD_DIM)),
            gather_rows(lambda r: r[1][:, :kv_w], lo, hi, lead + (KV_A, HEAD_DIM)),
            gather_rows(lambda r: r[2][:, :IDX_DIM], lo, hi, lead + (IDX_DIM,)),
            gather_rows(lambda r: r[0][:, kv_w:], lo, hi, lead + (KV_B, HEAD_DIM)),
            gather_rows(lambda r: r[1][:, kv_w:], lo, hi, lead + (KV_B, HEAD_DIM)),
        ]
    return tuple(outs)
```

```python
import functools

import jax
import jax.numpy as jnp
from jax import lax
from jax.experimental import pallas as pl
from jax.experimental.pallas import tpu as pltpu

HEAD_DIM = 128
KV_A = 4
IDX_HEADS = 32
IDX_DIM = 64
DSA_TOPK = 256
KV_B = 4
MOBA_BLOCK = 256
MOBA_TOPK = 3
N_GROUPS = 4
EXPERTS_PER_GROUP = 8
EXPERT_TOPK = 2
ROPE_THETA = 10000.0
LN_EPS = 1e-5

LANE = 128
ROW_ALIGN = 256
MXU_DTYPE = jnp.bfloat16
NEG_BIAS = -1e30
INT_MIN = -2147483648
VMEM_LIMIT = 56 * 1024 * 1024
PAGES_PER_CHUNK = 16
MOE_ROWS = 128
MOE_CHUNKS = 5
GATHER_DMA_PRIORITY = 1
MOE_KT = 1024
MOE_NT = 1024


def _cparams(*sem):
    return pltpu.CompilerParams(dimension_semantics=sem, vmem_limit_bytes=VMEM_LIMIT)


def _round_up(n, m):
    return (n + m - 1) // m * m


def _pick_tile(n, cap, mult):
    best = None
    for t in range(mult, cap + 1, mult):
        if n % t == 0:
            best = t
    assert best is not None, (n, cap, mult)
    return best


def _dot_nt(a, b):
    return lax.dot_general(a, b, (((1,), (1,)), ((), ())), preferred_element_type=jnp.float32)


def _ln_kernel(*refs, mode, alpha, router, pack_kt):
    it = iter(refs)
    x_ref = next(it)
    a_ref = next(it) if mode == "add" else None
    g_ref = next(it) if mode == "ple" else None
    p_ref = next(it) if mode == "ple" else None
    gam_ref, bet_ref = next(it), next(it)
    wh_ref = next(it) if router else None
    wl_ref = next(it) if router else None
    o32_ref, o16_ref = next(it), next(it)
    lg_ref = next(it) if router else None

    v = x_ref[...]
    if mode == "add":
        v = alpha * v + a_ref[...]
    elif mode == "ple":
        v = alpha * v + jax.nn.sigmoid(g_ref[...]) * p_ref[...]
    mu = jnp.mean(v, axis=-1, keepdims=True)
    d = v - mu
    var = jnp.mean(d * d, axis=-1, keepdims=True)
    y = d * lax.rsqrt(var + LN_EPS) * gam_ref[...] + bet_ref[...]
    o32_ref[...] = y
    if pack_kt is not None:
        kt = pack_kt
        for b in range(o16_ref.shape[1] // kt):
            lo = y[:, 2 * b * kt:(2 * b + 1) * kt].astype(jnp.bfloat16).astype(jnp.float32)
            hi = y[:, (2 * b + 1) * kt:(2 * b + 2) * kt].astype(jnp.bfloat16).astype(jnp.float32)
            o16_ref[:, b * kt:(b + 1) * kt] = (
                lax.shift_right_logical(lax.bitcast_convert_type(lo, jnp.uint32), jnp.uint32(16))
                | (lax.bitcast_convert_type(hi, jnp.uint32) & jnp.uint32(0xFFFF0000)))
    else:
        o16_ref[...] = y.astype(o16_ref.dtype)
    if router:
        yh = y.astype(MXU_DTYPE)
        yl = (y - yh.astype(jnp.float32)).astype(MXU_DTYPE)
        wh = wh_ref[...]
        lg = jnp.dot(yh, wh, preferred_element_type=jnp.float32)
        lg = lg + jnp.dot(yl, wh, preferred_element_type=jnp.float32)
        lg = lg + jnp.dot(yh, wl_ref[...], preferred_element_type=jnp.float32)
        lg_ref[...] = lg


def _ln_embed_kernel(xh_ref, xt_ref, gam_ref, bet_ref, o32_ref, o16_ref, *, n_head):
    v = jnp.where(pl.program_id(0) < n_head, xh_ref[...], xt_ref[...])
    mu = jnp.mean(v, axis=-1, keepdims=True)
    d = v - mu
    var = jnp.mean(d * d, axis=-1, keepdims=True)
    y = d * lax.rsqrt(var + LN_EPS) * gam_ref[...] + bet_ref[...]
    o32_ref[...] = y
    o16_ref[...] = y.astype(o16_ref.dtype)


def _layer_norm_embed(x_head, x_tail, gam, bet):
    n_h, D = x_head.shape
    n_t = x_tail.shape[0]
    tr = _pick_tile(ROW_ALIGN, 128, 16)
    assert n_h % tr == 0 and n_t % tr == 0
    nh, ntl = n_h // tr, n_t // tr
    vec = pl.BlockSpec((1, D), lambda i: (0, 0))
    row = pl.BlockSpec((tr, D), lambda i: (i, 0))
    return pl.pallas_call(
        functools.partial(_ln_embed_kernel, n_head=nh), grid=(nh + ntl,),
        in_specs=[pl.BlockSpec((tr, D), lambda i: (jnp.minimum(i, nh - 1), 0)),
                  pl.BlockSpec((tr, D), lambda i: (jnp.maximum(i - nh, 0), 0)), vec, vec],
        out_specs=[row, row],
        out_shape=[jax.ShapeDtypeStruct((n_h + n_t, D), jnp.float32),
                   jax.ShapeDtypeStruct((n_h + n_t, D), MXU_DTYPE)],
        compiler_params=_cparams("parallel"), name="ln_embed")(x_head, x_tail, gam.reshape(1, D), bet.reshape(1, D))


def _layer_norm(xs, gam, bet, mode, alpha=1.0, router_w=None, pack_kt=None):
    Np, D = xs[0].shape
    tr = _pick_tile(Np, 192, 16)
    row = pl.BlockSpec((tr, D), lambda i: (i, 0))
    vec = pl.BlockSpec((1, D), lambda i: (0, 0))
    in_specs = [row] * len(xs) + [vec, vec]
    args = list(xs) + [gam.reshape(1, D), bet.reshape(1, D)]
    if pack_kt is None:
        out_shape = [jax.ShapeDtypeStruct((Np, D), jnp.float32), jax.ShapeDtypeStruct((Np, D), MXU_DTYPE)]
        out_specs = [row, row]
    else:
        assert D % (2 * pack_kt) == 0
        out_shape = [jax.ShapeDtypeStruct((Np, D), jnp.float32), jax.ShapeDtypeStruct((Np, D // 2), jnp.uint32)]
        out_specs = [row, pl.BlockSpec((tr, D // 2), lambda i: (i, 0))]
    if router_w is not None:
        wspec = pl.BlockSpec((D, LANE), lambda i: (0, 0))
        in_specs += [wspec, wspec]
        args += list(router_w)
        out_shape.append(jax.ShapeDtypeStruct((Np, LANE), jnp.float32))
        out_specs.append(pl.BlockSpec((tr, LANE), lambda i: (i, 0)))
    return pl.pallas_call(
        functools.partial(_ln_kernel, mode=mode, alpha=alpha, router=router_w is not None, pack_kt=pack_kt),
        grid=(Np // tr,), in_specs=in_specs, out_specs=out_specs, out_shape=out_shape,
        compiler_params=_cparams("parallel"), name="ln_" + mode)(*args)


def _mm_kernel(*refs, n_x, mode, slice_major):
    x_refs = refs[:n_x]
    w_refs = refs[n_x:2 * n_x]
    rest = refs[2 * n_x:]
    if mode == "none":
        (o_ref,) = rest
    else:
        cos_ref, sin_ref, o_ref = rest
    y = jnp.dot(x_refs[0][...], w_refs[0][...], preferred_element_type=jnp.float32)
    for k in range(1, n_x):
        y = y + jnp.dot(x_refs[k][...], w_refs[k][...], preferred_element_type=jnp.float32)
    if mode == "none" and not slice_major:
        o_ref[...] = y.astype(o_ref.dtype)
        return
    tm = y.shape[0]
    if mode != "none":
        cos = cos_ref[...]
        sin = sin_ref[...]
    if mode == "rope64":
        lane = lax.broadcasted_iota(jnp.int32, (tm, LANE), 1)
        first_half = (lane % 64) < 32
    for s in range(y.shape[1] // LANE):
        yh = y[:, s * LANE:(s + 1) * LANE]
        if mode == "rope128":
            yh = yh * cos + pltpu.roll(yh, 64, 1) * sin
        elif mode == "rope64":
            partner = jnp.where(first_half, pltpu.roll(yh, 96, 1), pltpu.roll(yh, 32, 1))
            yh = yh * cos + partner * sin
        if slice_major:
            o_ref[s] = yh.astype(o_ref.dtype)
        else:
            o_ref[:, s * LANE:(s + 1) * LANE] = yh.astype(o_ref.dtype)


def _matmul(xs, ws, out_dtype, mode="none", tables=None, slice_major=False, name="mm"):
    Np = xs[0].shape[0]
    Nc = ws[0].shape[1]
    tm = _pick_tile(Np, 640, 16)
    tn = _pick_tile(Nc, 1024, LANE)
    in_specs = [pl.BlockSpec((tm, x.shape[1]), lambda j, i: (i, 0)) for x in xs]
    in_specs += [pl.BlockSpec((w.shape[0], tn), lambda j, i: (0, j)) for w in ws]
    args = list(xs) + list(ws)
    if mode != "none":
        in_specs += [pl.BlockSpec((tm, LANE), lambda j, i: (i, 0))] * 2
        args += list(tables)
    if slice_major:
        out_shape = jax.ShapeDtypeStruct((Nc // LANE, Np, LANE), out_dtype)
        out_spec = pl.BlockSpec((tn // LANE, tm, LANE), lambda j, i: (j, i, 0))
    else:
        out_shape = jax.ShapeDtypeStruct((Np, Nc), out_dtype)
        out_spec = pl.BlockSpec((tm, tn), lambda j, i: (i, j))
    return pl.pallas_call(
        functools.partial(_mm_kernel, n_x=len(xs), mode=mode, slice_major=slice_major),
        grid=(Nc // tn, Np // tm), in_specs=in_specs, out_specs=out_spec, out_shape=out_shape,
        compiler_params=_cparams("parallel", "parallel"), name=name)(*args)


def _float_keys(x):
    b = lax.bitcast_convert_type(x, jnp.int32)
    return jnp.where(b < 0, b ^ jnp.int32(0x7FFFFFFF), b)


def _kth_largest(count_ge, shape, k):
    def body(it, ans):
        cand = ans + lax.shift_left(jnp.int32(1), 31 - it)
        return jnp.where(count_ge(cand) >= k, cand, ans)
    return lax.fori_loop(0, 32, body, jnp.full(shape, INT_MIN, jnp.int32))


def _topk_lanes(gate, k):
    lane = lax.broadcasted_iota(jnp.int32, gate.shape, 1)
    sel = jnp.zeros(gate.shape, jnp.float32)
    g = gate
    for _ in range(k):
        m = jnp.max(g, axis=-1, keepdims=True)
        idx = jnp.min(jnp.where(g == m, lane, LANE), axis=-1, keepdims=True)
        hit = lane == idx
        sel = jnp.where(hit, 1.0, sel)
        g = jnp.where(hit, -jnp.inf, g)
    return sel


def _attend(q_ref, k_ref, v_ref, bias_ref, o_ref, rep, scale, width):
    k = k_ref[:width, :].astype(MXU_DTYPE)
    v = v_ref[:width, :].astype(MXU_DTYPE)
    bias = bias_ref[:, :width]
    for r in range(rep):
        s = _dot_nt(q_ref[r], k) * scale + bias
        m = jnp.max(s, axis=-1, keepdims=True)
        p = jnp.exp(s - m)
        l = jnp.sum(p, axis=-1, keepdims=True)
        o = jnp.dot(p.astype(MXU_DTYPE), v, preferred_element_type=jnp.float32) / l
        o_ref[:, r * HEAD_DIM:(r + 1) * HEAD_DIM] = o.astype(o_ref.dtype)


def _causal_extents(n_tiles, max_branches=4):
    nbr = min(max_branches, n_tiles)
    out, lo = [], 0
    for hi in sorted({-(-n_tiles * (b + 1) // nbr) for b in range(nbr)}):
        out.append((lo, hi))
        lo = hi
    return out


def _dsa_select(qi_ref, wq_ref, kiw_ref, bias_ref, key_ref, i, tq, W, n_keep):
    klane = lax.broadcasted_iota(jnp.int32, (W, LANE), 1)
    ka32 = jnp.where(klane < IDX_DIM, kiw_ref[:W, :], 0.0)
    ka = ka32.astype(MXU_DTYPE)
    kb = pltpu.roll(ka32, IDX_DIM, 1).astype(MXU_DTYPE)
    wq = wq_ref[...]
    wlane = lax.broadcasted_iota(jnp.int32, (tq, LANE), 1)

    def head_weight(h):
        return jnp.sum(jnp.where(wlane == IDX_DIM + h, wq, 0.0), axis=-1, keepdims=True)

    def pair(hp, carry):
        qp = qi_ref[hp]
        c = (jnp.maximum(_dot_nt(qp, ka), 0.0) * head_weight(2 * hp)
             + jnp.maximum(_dot_nt(qp, kb), 0.0) * head_weight(2 * hp + 1))

        bias_ref[:, :W] += c
        return carry
    bias_ref[:, :W] = jnp.zeros((tq, W), jnp.float32)
    lax.fori_loop(0, IDX_HEADS // 2, pair, 0)
    score = bias_ref[:, :W]
    row = lax.broadcasted_iota(jnp.int32, (tq, W), 0) + i * tq
    col = lax.broadcasted_iota(jnp.int32, (tq, W), 1)
    causal = col <= row
    key_ref[:, :W] = jnp.where(causal, _float_keys(score), INT_MIN)

    def count_ge(cand):
        return jnp.sum(jnp.where(key_ref[:, :W] >= cand, 1, 0), axis=-1, keepdims=True)

    thr = _kth_largest(count_ge, (tq, 1), n_keep)
    keys = key_ref[:, :W]
    gt = keys > thr
    eq = keys == thr
    n_gt = jnp.sum(jnp.where(gt, 1, 0), axis=-1, keepdims=True)
    n_eq = jnp.sum(jnp.where(eq, 1, 0), axis=-1, keepdims=True)
    bias_ref[:, :W] = jnp.where((gt | eq) & causal, 0.0, NEG_BIAS)
    tie = (thr > INT_MIN) & (n_gt + n_eq > n_keep)

    @pl.when(jnp.max(jnp.where(tie, 1, 0)) > 0)
    def _():
        need = n_keep - n_gt
        nbits = W.bit_length()

        def body(it, lim):
            cand = lim + lax.shift_left(jnp.int32(1), nbits - 1 - it)
            c = jnp.sum(jnp.where(eq & (col < cand), 1, 0), axis=-1, keepdims=True)
            return jnp.where(c <= need, cand, lim)

        lim = lax.fori_loop(0, nbits, body, jnp.zeros((tq, 1), jnp.int32))
        bias_ref[:, :W] = jnp.where((gt | (eq & (col < lim))) & causal, 0.0, NEG_BIAS)


def _dsa_prompt_kernel(q_ref, k_ref, v_ref, qi_ref, wq_ref, kiw_ref, o_ref, bias_ref, key_ref,
                       *, tq, T, rep, n_keep, scale):
    i = pl.program_id(1)
    g = pl.program_id(2)
    for lo, hi in _causal_extents(T // tq):
        @pl.when((i >= lo) & (i < hi))
        def _(W=hi * tq):
            @pl.when(g == 0)
            def _():
                _dsa_select(qi_ref, wq_ref, kiw_ref, bias_ref, key_ref, i, tq, W, n_keep)

            _attend(q_ref, k_ref, v_ref, bias_ref, o_ref, rep, scale, W)


def _dsa_prompt(q128, k128, v128, qi, kiwi, B, T, rep):
    tq = min(256, T)
    assert T % tq == 0 and tq % LANE == 0
    nT = T // tq
    n_keep = min(DSA_TOPK, T // 4)
    kern = functools.partial(_dsa_prompt_kernel, tq=tq, T=T, rep=rep, n_keep=n_keep,
                             scale=HEAD_DIM ** -0.5)
    return pl.pallas_call(
        kern, grid=(B, nT, KV_A),
        in_specs=[
            pl.BlockSpec((rep, tq, HEAD_DIM), lambda b, i, g: (g, b * nT + i, 0)),
            pl.BlockSpec((T, HEAD_DIM), lambda b, i, g: (b, g)),
            pl.BlockSpec((T, HEAD_DIM), lambda b, i, g: (b, g)),
            pl.BlockSpec((IDX_HEADS // 2, tq, LANE), lambda b, i, g: (0, b * nT + i, 0)),
            pl.BlockSpec((tq, LANE), lambda b, i, g: (b * nT + i, 0)),
            pl.BlockSpec((T, LANE), lambda b, i, g: (b, 0)),
        ],
        out_specs=pl.BlockSpec((tq, rep * HEAD_DIM), lambda b, i, g: (b * nT + i, g)),
        out_shape=jax.ShapeDtypeStruct((B * T, KV_A * rep * HEAD_DIM), MXU_DTYPE),
        scratch_shapes=[pltpu.VMEM((tq, T), jnp.float32), pltpu.VMEM((tq, T), jnp.int32)],
        compiler_params=_cparams("parallel", "parallel", "arbitrary"), name="dsa_prompt",
    )(q128, k128, v128, qi, kiwi, kiwi)


def _moba_prompt_kernel(q_ref, k_ref, v_ref, o_ref, bias_ref, *, bs, nblk, rep, n_pick, scale):
    i = pl.program_id(1)
    qsum = q_ref[0].astype(jnp.float32)
    for r in range(1, rep):
        qsum = qsum + q_ref[r].astype(jnp.float32)
    lane = lax.broadcasted_iota(jnp.int32, (bs, LANE), 1)
    past = lane < i
    row = lax.broadcasted_iota(jnp.int32, (bs, bs), 0)
    col = lax.broadcasted_iota(jnp.int32, (bs, bs), 1)
    own_bias = jnp.where(col <= row, 0.0, NEG_BIAS)
    for lo, hi in _causal_extents(nblk):
        @pl.when((i >= lo) & (i < hi))
        def _(nb=hi):
            gate = jnp.full((bs, LANE), -jnp.inf, jnp.float32)
            for n in range(nb):
                mean_n = jnp.mean(k_ref[n * bs:(n + 1) * bs, :], axis=0, keepdims=True)
                gate = jnp.where(lane == n, jnp.sum(qsum * mean_n, axis=-1, keepdims=True), gate)
            sel = jnp.where(past, _topk_lanes(jnp.where(past, gate, -jnp.inf), n_pick), 0.0)
            for n in range(nb):
                picked = jnp.where(sel[:, n:n + 1] > 0.5, 0.0, NEG_BIAS)
                bias_ref[:, n * bs:(n + 1) * bs] = jnp.where(i == n, own_bias, jnp.broadcast_to(picked, (bs, bs)))
            _attend(q_ref, k_ref, v_ref, bias_ref, o_ref, rep, scale, nb * bs)


def _moba_prompt(q128, k128, v128, B, T, rep, h_a):
    bs = MOBA_BLOCK
    assert T % bs == 0
    nblk = T // bs
    assert 1 <= nblk <= LANE
    kern = functools.partial(_moba_prompt_kernel, bs=bs, nblk=nblk, rep=rep,
                             n_pick=min(MOBA_TOPK, nblk), scale=HEAD_DIM ** -0.5)
    return pl.pallas_call(
        kern, grid=(B, nblk, KV_B),
        in_specs=[
            pl.BlockSpec((rep, bs, HEAD_DIM), lambda b, i, g: (h_a // rep + g, b * nblk + i, 0)),
            pl.BlockSpec((T, HEAD_DIM), lambda b, i, g: (b, KV_A + g)),
            pl.BlockSpec((T, HEAD_DIM), lambda b, i, g: (b, KV_A + g)),
        ],
        out_specs=pl.BlockSpec((bs, rep * HEAD_DIM), lambda b, i, g: (b * nblk + i, g)),
        out_shape=jax.ShapeDtypeStruct((B * T, KV_B * rep * HEAD_DIM), MXU_DTYPE),
        scratch_shapes=[pltpu.VMEM((bs, T), jnp.float32)],
        compiler_params=_cparams("parallel", "parallel", "parallel"), name="moba_prompt",
    )(q128, k128, v128)


def _page_copies(pt_ref, pool_ref, layer, buf, sem, b, chunk, slot, pages, page, groups=0):
    out = []
    for p in range(pages):
        phys = pt_ref[b, chunk * pages + p]
        rows = pl.ds(p * page, page)
        if not groups:
            out.append(pltpu.make_async_copy(pool_ref.at[layer, phys], buf.at[slot, rows], sem.at[slot]))
        for g in range(groups):
            out.append(pltpu.make_async_copy(pool_ref.at[layer, phys, :, g, :], buf.at[slot, g, rows], sem.at[slot]))
    return out


def _paged_step(c, nchunk, start_fn, wait_fn):
    @pl.when(c == 0)
    def _():
        start_fn(0, 0)

    slot = c % 2
    wait_fn(c, slot)

    @pl.when(c + 1 < nchunk)
    def _():
        start_fn(c + 1, 1 - slot)

    return slot


def _dsa_decode_mask_kernel(pt_ref, qi_ref, wc_ref, pool_ref, knew_ref, bp_ref, bn_ref,
                            kbuf, sem, sp_ref, sn_ref, *, layer, nchunk, pages, page, Td, n_keep, past):
    b = pl.program_id(0)
    c = pl.program_id(1)
    CH = pages * page

    def start_fn(cc, slot):
        for cp in _page_copies(pt_ref, pool_ref, layer, kbuf, sem, b, cc, slot, pages, page):
            cp.start()

    def wait_fn(cc, slot):
        for cp in _page_copies(pt_ref, pool_ref, layer, kbuf, sem, b, cc, slot, pages, page):
            cp.wait()

    def scores(kc, width):
        d = _dot_nt(qi_ref[0], kc.astype(MXU_DTYPE))
        r = jnp.maximum(d, 0.0) * wc_ref[0]
        return jnp.sum(r.reshape(IDX_HEADS, Td, width), axis=0)

    @pl.when(c < nchunk)
    def _():
        slot = _paged_step(c, nchunk, start_fn, wait_fn)
        sp_ref[c] = scores(kbuf[slot], CH)

    @pl.when(c == nchunk)
    def _():
        sn_ref[...] = scores(knew_ref[0], LANE)
        trow = lax.broadcasted_iota(jnp.int32, (Td, LANE), 0)
        ncol = lax.broadcasted_iota(jnp.int32, (Td, LANE), 1)
        new_ok = ncol <= trow
        kp = _float_keys(sp_ref[...])
        kn = jnp.where(new_ok, _float_keys(sn_ref[...]), INT_MIN)

        def count(mp, mn):
            return (jnp.sum(jnp.sum(jnp.where(mp, 1, 0), axis=0), axis=-1, keepdims=True)
                    + jnp.sum(jnp.where(mn, 1, 0), axis=-1, keepdims=True))

        thr = _kth_largest(lambda cand: count(kp >= cand[None], kn >= cand), (Td, 1), n_keep)
        gt_p, eq_p = kp > thr[None], kp == thr[None]
        gt_n, eq_n = kn > thr, kn == thr
        n_gt = count(gt_p, gt_n)
        n_eq = count(eq_p, eq_n)
        bp_ref[0] = jnp.where(gt_p | eq_p, 0.0, NEG_BIAS)
        bn_ref[0] = jnp.where((gt_n | eq_n) & new_ok, 0.0, NEG_BIAS)
        tie = (thr > INT_MIN) & (n_gt + n_eq > n_keep)

        @pl.when(jnp.max(jnp.where(tie, 1, 0)) > 0)
        def _():
            need = n_keep - n_gt
            pos_p = (lax.broadcasted_iota(jnp.int32, (nchunk, Td, CH), 0) * CH
                     + lax.broadcasted_iota(jnp.int32, (nchunk, Td, CH), 2))
            pos_n = past + ncol
            nbits = (past + LANE).bit_length()

            def body(it, lim):
                cand = lim + lax.shift_left(jnp.int32(1), nbits - 1 - it)
                cnt = count(eq_p & (pos_p < cand[None]), eq_n & (pos_n < cand))
                return jnp.where(cnt <= need, cand, lim)

            lim = lax.fori_loop(0, nbits, body, jnp.zeros((Td, 1), jnp.int32))
            bp_ref[0] = jnp.where(gt_p | (eq_p & (pos_p < lim[None])), 0.0, NEG_BIAS)
            bn_ref[0] = jnp.where((gt_n | (eq_n & (pos_n < lim))) & new_ok, 0.0, NEG_BIAS)


def _dsa_decode_mask(page_table, qi_s, wcol, pool, knew, layer, Td):
    Bd, n_pages = page_table.shape
    page = pool.shape[2]
    pages = min(PAGES_PER_CHUNK, n_pages)
    assert n_pages % pages == 0
    nchunk = n_pages // pages
    CH = pages * page
    past = n_pages * page
    n_keep = min(DSA_TOPK, (past + Td) // 4)
    HT = qi_s.shape[1]
    kern = functools.partial(_dsa_decode_mask_kernel, layer=layer, nchunk=nchunk, pages=pages, page=page,
                             Td=Td, n_keep=n_keep, past=past)
    grid_spec = pltpu.PrefetchScalarGridSpec(
        num_scalar_prefetch=1, grid=(Bd, nchunk + 1),
        in_specs=[
            pl.BlockSpec((1, HT, IDX_DIM), lambda b, c, pt: (b, 0, 0)),
            pl.BlockSpec((1, HT, 1), lambda b, c, pt: (b, 0, 0)),
            pl.BlockSpec(memory_space=pl.ANY),
            pl.BlockSpec((1, LANE, IDX_DIM), lambda b, c, pt: (b, 0, 0)),
        ],
        out_specs=[
            pl.BlockSpec((1, nchunk, Td, CH), lambda b, c, pt: (b, 0, 0, 0)),
            pl.BlockSpec((1, Td, LANE), lambda b, c, pt: (b, 0, 0)),
        ],
        scratch_shapes=[
            pltpu.VMEM((2, CH, IDX_DIM), jnp.float32),
            pltpu.SemaphoreType.DMA((2,)),
            pltpu.VMEM((nchunk, Td, CH), jnp.float32),
            pltpu.VMEM((Td, LANE), jnp.float32),
        ])
    return pl.pallas_call(
        kern, grid_spec=grid_spec,
        out_shape=[jax.ShapeDtypeStruct((Bd, nchunk, Td, CH), jnp.float32),
                   jax.ShapeDtypeStruct((Bd, Td, LANE), jnp.float32)],
        compiler_params=_cparams("arbitrary", "arbitrary"), name="dsa_decode_mask",
    )(page_table, qi_s, wcol, pool, knew)


def _moba_decode_mask_kernel(pt_ref, q_ref, pool_ref, bp_ref, kbuf, sem, gate_ref,
                             *, layer, nchunk, pages, page, Td, rep, bs, n_pick):
    b = pl.program_id(0)
    c = pl.program_id(1)
    CH = pages * page
    per = CH // bs
    nblk = nchunk * per

    def start_fn(cc, slot):
        for cp in _page_copies(pt_ref, pool_ref, layer, kbuf, sem, b, cc, slot, pages, page, KV_B):
            cp.start()

    def wait_fn(cc, slot):
        for cp in _page_copies(pt_ref, pool_ref, layer, kbuf, sem, b, cc, slot, pages, page, KV_B):
            cp.wait()

    @pl.when(c == 0)
    def _():
        gate_ref[...] = jnp.full(gate_ref.shape, -jnp.inf, jnp.float32)

    slot = _paged_step(c, nchunk, start_fn, wait_fn)
    lane = lax.broadcasted_iota(jnp.int32, (Td, LANE), 1)
    for g in range(KV_B):
        qsum = jnp.sum(q_ref[0, g].astype(jnp.float32).reshape(rep, Td, HEAD_DIM), axis=0)
        gate = gate_ref[g]
        for nb in range(per):
            mean_nb = jnp.mean(kbuf[slot, g, nb * bs:(nb + 1) * bs, :], axis=0, keepdims=True)
            gate = jnp.where(lane == c * per + nb, jnp.sum(qsum * mean_nb, axis=-1, keepdims=True), gate)
        gate_ref[g] = gate

    @pl.when(c == nchunk - 1)
    def _():
        for g in range(KV_B):
            sel = jnp.where(lane < nblk, _topk_lanes(gate_ref[g], n_pick), 0.0)
            for n in range(nblk):
                picked = jnp.where(sel[:, n:n + 1] > 0.5, 0.0, NEG_BIAS)
                bp_ref[0, n // per, g, :, (n % per) * bs:(n % per + 1) * bs] = jnp.broadcast_to(picked, (Td, bs))


def _moba_decode_mask(page_table, q_s, pool, layer, Td, rep):
    Bd, n_pages = page_table.shape
    page = pool.shape[2]
    pages = min(PAGES_PER_CHUNK, n_pages)
    assert n_pages % pages == 0
    nchunk = n_pages // pages
    CH = pages * page
    bs = MOBA_BLOCK
    assert CH % bs == 0 and Td <= bs
    nblk = n_pages * page // bs
    assert nblk <= LANE
    kern = functools.partial(_moba_decode_mask_kernel, layer=layer, nchunk=nchunk, pages=pages, page=page,
                             Td=Td, rep=rep, bs=bs, n_pick=min(MOBA_TOPK, (n_pages * page + Td) // bs))
    grid_spec = pltpu.PrefetchScalarGridSpec(
        num_scalar_prefetch=1, grid=(Bd, nchunk),
        in_specs=[
            pl.BlockSpec((1, KV_B, rep * Td, HEAD_DIM), lambda b, c, pt: (b, 0, 0, 0)),
            pl.BlockSpec(memory_space=pl.ANY),
        ],
        out_specs=pl.BlockSpec((1, nchunk, KV_B, Td, CH), lambda b, c, pt: (b, 0, 0, 0, 0)),
        scratch_shapes=[
            pltpu.VMEM((2, KV_B, CH, HEAD_DIM), jnp.float32),
            pltpu.SemaphoreType.DMA((2,)),
            pltpu.VMEM((KV_B, Td, LANE), jnp.float32),
        ])
    return pl.pallas_call(
        kern, grid_spec=grid_spec,
        out_shape=jax.ShapeDtypeStruct((Bd, nchunk, KV_B, Td, CH), jnp.float32),
        compiler_params=_cparams("arbitrary", "arbitrary"), name="moba_decode_mask",
    )(page_table, q_s, pool)


def _paged_attn_kernel(pt_ref, q_ref, kpool_ref, vpool_ref, knew_ref, vnew_ref, bp_ref, bn_ref, o_ref,
                       kbuf, vbuf, sem, m_ref, l_ref, acc_ref,
                       *, layer, nchunk, pages, page, Td, rep, n_kv, per_group_bias, scale):
    b = pl.program_id(0)
    c = pl.program_id(1)

    def start_fn(cc, slot):
        for cp in (_page_copies(pt_ref, kpool_ref, layer, kbuf, sem.at[0], b, cc, slot, pages, page, n_kv)
                   + _page_copies(pt_ref, vpool_ref, layer, vbuf, sem.at[1], b, cc, slot, pages, page, n_kv)):
            cp.start()

    def wait_fn(cc, slot):
        for cp in (_page_copies(pt_ref, kpool_ref, layer, kbuf, sem.at[0], b, cc, slot, pages, page, n_kv)
                   + _page_copies(pt_ref, vpool_ref, layer, vbuf, sem.at[1], b, cc, slot, pages, page, n_kv)):
            cp.wait()

    @pl.when(c == 0)
    def _():
        m_ref[...] = jnp.full(m_ref.shape, -jnp.inf, jnp.float32)
        l_ref[...] = jnp.zeros(l_ref.shape, jnp.float32)
        acc_ref[...] = jnp.zeros(acc_ref.shape, jnp.float32)

    def process(k_fn, v_fn, bias_fn):
        for g in range(n_kv):
            k = k_fn(g).astype(MXU_DTYPE)
            v = v_fn(g).astype(MXU_DTYPE)
            bias = bias_fn(g)
            s = _dot_nt(q_ref[0, g], k) * scale + jnp.concatenate([bias] * rep, axis=0)
            m_old = m_ref[g]
            m_new = jnp.maximum(m_old, jnp.max(s, axis=-1, keepdims=True))
            a = jnp.exp(m_old - m_new)
            p = jnp.exp(s - m_new)
            l_ref[g] = a * l_ref[g] + jnp.sum(p, axis=-1, keepdims=True)
            acc_ref[g] = a * acc_ref[g] + jnp.dot(p.astype(MXU_DTYPE), v, preferred_element_type=jnp.float32)
            m_ref[g] = m_new

    @pl.when(c < nchunk)
    def _():
        slot = _paged_step(c, nchunk, start_fn, wait_fn)
        process(lambda g: kbuf[slot, g], lambda g: vbuf[slot, g],
                lambda g: bp_ref[0, 0, g if per_group_bias else 0])

    @pl.when(c == nchunk)
    def _():
        process(lambda g: knew_ref[0, :, g * HEAD_DIM:(g + 1) * HEAD_DIM],
                lambda g: vnew_ref[0, :, g * HEAD_DIM:(g + 1) * HEAD_DIM], lambda g: bn_ref[0])
        for g in range(n_kv):
            o_ref[0, g] = acc_ref[g] / l_ref[g]


def _paged_attention(page_table, q_s, kpool, vpool, knew, vnew, bias_past, bias_new, layer, Td, rep):
    Bd, n_pages = page_table.shape
    page = kpool.shape[2]
    pages = min(PAGES_PER_CHUNK, n_pages)
    nchunk = n_pages // pages
    CH = pages * page
    n_kv = q_s.shape[1]
    R = rep * Td
    W = n_kv * HEAD_DIM
    gb = bias_past.shape[2]
    nb_new = bias_new.shape[0]
    kern = functools.partial(_paged_attn_kernel, layer=layer, nchunk=nchunk, pages=pages, page=page, Td=Td,
                             rep=rep, n_kv=n_kv, per_group_bias=gb > 1, scale=HEAD_DIM ** -0.5)
    grid_spec = pltpu.PrefetchScalarGridSpec(
        num_scalar_prefetch=1, grid=(Bd, nchunk + 1),
        in_specs=[
            pl.BlockSpec((1, n_kv, R, HEAD_DIM), lambda b, c, pt: (b, 0, 0, 0)),
            pl.BlockSpec(memory_space=pl.ANY),
            pl.BlockSpec(memory_space=pl.ANY),
            pl.BlockSpec((1, LANE, W), lambda b, c, pt: (b, 0, 0)),
            pl.BlockSpec((1, LANE, W), lambda b, c, pt: (b, 0, 0)),
            pl.BlockSpec((1, 1, gb, Td, CH), lambda b, c, pt: (b, jnp.minimum(c, nchunk - 1), 0, 0, 0)),
            pl.BlockSpec((1, Td, LANE), lambda b, c, pt: (b if nb_new > 1 else 0, 0, 0)),
        ],
        out_specs=pl.BlockSpec((1, n_kv, R, HEAD_DIM), lambda b, c, pt: (b, 0, 0, 0)),
        scratch_shapes=[
            pltpu.VMEM((2, n_kv, CH, HEAD_DIM), jnp.float32),
            pltpu.VMEM((2, n_kv, CH, HEAD_DIM), jnp.float32),
            pltpu.SemaphoreType.DMA((2, 2)),
            pltpu.VMEM((n_kv, R, 1), jnp.float32),
            pltpu.VMEM((n_kv, R, 1), jnp.float32),
            pltpu.VMEM((n_kv, R, HEAD_DIM), jnp.float32),
        ])
    return pl.pallas_call(
        kern, grid_spec=grid_spec,
        out_shape=jax.ShapeDtypeStruct((Bd, n_kv, R, HEAD_DIM), jnp.float32),
        compiler_params=_cparams("arbitrary", "arbitrary"), name="paged_attention",
    )(page_table, q_s, kpool, vpool, knew, vnew, bias_past, bias_new)


def _route_kernel(lg_ref, eid_ref, gate_ref):
    lg = lg_ref[...]
    lane = lax.broadcasted_iota(jnp.int32, lg.shape, 1)
    gmask = lane < N_GROUPS
    gl = jnp.where(gmask, lg, -jnp.inf)
    gmax = jnp.max(gl, axis=-1, keepdims=True)
    gsel = jnp.min(jnp.where(gl == gmax, lane, LANE), axis=-1, keepdims=True)
    g_w = 1.0 / jnp.sum(jnp.where(gmask, jnp.exp(gl - gmax), 0.0), axis=-1, keepdims=True)
    lo = N_GROUPS + gsel * EXPERTS_PER_GROUP
    el = jnp.where((lane >= lo) & (lane < lo + EXPERTS_PER_GROUP), lg, -jnp.inf)
    v1 = jnp.max(el, axis=-1, keepdims=True)
    i1 = jnp.min(jnp.where(el == v1, lane, LANE), axis=-1, keepdims=True)
    el2 = jnp.where(lane == i1, -jnp.inf, el)
    v2 = jnp.max(el2, axis=-1, keepdims=True)
    i2 = jnp.min(jnp.where(el2 == v2, lane, LANE), axis=-1, keepdims=True)
    e2 = jnp.exp(v2 - v1)
    p1 = 1.0 / (1.0 + e2)
    eid_ref[...] = jnp.where(lane == 0, i1 - N_GROUPS, jnp.where(lane == 1, i2 - N_GROUPS, 0))
    gate_ref[...] = jnp.where(lane == 0, g_w * p1, jnp.where(lane == 1, g_w * (e2 * p1), 0.0))


def _route(logits):
    Np = logits.shape[0]
    tr = _pick_tile(Np, 1024, 8)
    spec = pl.BlockSpec((tr, LANE), lambda i: (i, 0))
    return pl.pallas_call(
        _route_kernel, grid=(Np // tr,), in_specs=[spec], out_specs=[spec, spec],
        out_shape=[jax.ShapeDtypeStruct((Np, LANE), jnp.int32), jax.ShapeDtypeStruct((Np, LANE), jnp.float32)],
        compiler_params=_cparams("parallel"), name="route")(logits)


def _moe_plan(eid, n_experts, rc, nch_max):
    M = eid.shape[0]
    C = rc * nch_max
    order = jnp.argsort(eid, stable=True).astype(jnp.int32)
    eid_s = eid[order]
    counts = jnp.bincount(eid, length=n_experts).astype(jnp.int32)
    start = jnp.cumsum(counts) - counts
    pcounts = (counts + rc - 1) // rc * rc
    pstart = jnp.cumsum(pcounts) - pcounts
    slot_s = (pstart[eid_s] + jnp.arange(M, dtype=jnp.int32) - start[eid_s]).astype(jnp.int32)
    R = _round_up(M, rc) + n_experts * rc
    pend = pstart + pcounts
    slots = jnp.arange(R, dtype=jnp.int32)
    e_slot = jnp.minimum(jnp.sum(pend[None, :] <= slots[:, None], axis=1), n_experts - 1)
    j_slot = slots - pstart[e_slot]
    src = jnp.clip(start[e_slot] + j_slot, 0, M - 1)
    tok = jnp.where((j_slot < counts[e_slot]) & (slots < pend[-1]), order[src] // EXPERT_TOPK, 0)
    dest = slot_s[jnp.argsort(order)]
    nseg = (pcounts + C - 1) // C
    send = jnp.cumsum(nseg)
    sstart = send - nseg
    n_seg = -(-R // C) + n_experts
    sidx = jnp.arange(n_seg, dtype=jnp.int32)
    e_of = jnp.minimum(jnp.sum(send[None, :] <= sidx[:, None], axis=1), n_experts - 1).astype(jnp.int32)
    active = sidx < send[-1]
    local = sidx - sstart[e_of]
    nch = jnp.where(active, jnp.clip((pcounts[e_of] - local * C + rc - 1) // rc, 0, nch_max), 0)
    used = pstart[-1] + pcounts[-1]
    idle_row0 = used + (sidx - send[-1]) * C
    row0 = jnp.where(active, pstart[e_of] + local * C, jnp.minimum(idle_row0, R))
    nzero = jnp.where(active, 0, jnp.clip((R - idle_row0) // rc, 0, nch_max))
    seg_e = jnp.where(active, e_of, e_of[jnp.maximum(send[-1] - 1, 0)])
    i32 = jnp.int32
    return tok, dest, seg_e.astype(i32), row0.astype(i32), nch.astype(i32), nzero.astype(i32), R


def _moe_kernel(seg_e_ref, row0_ref, nch_ref, nzero_ref, tok_ref, x_hbm, wg_ref, wu_ref, wd_ref, y_hbm,
                xp, a_acc, u_acc, hbuf, wgb, wub, wdb, ostage, gsem, osem, *, rc, nk, nn, n_seg):
    del seg_e_ref
    s = pl.program_id(0)
    j = pl.program_id(1)
    nch = nch_ref[s]
    row0 = row0_ref[s]
    nzero = nzero_ref[s]
    cur = s % 2
    kt = wgb.shape[0]

    def chunk_rows(ch):
        return pl.ds(pl.multiple_of(ch * rc, rc), rc)

    def out_copy(slot, ch, n):
        dst = y_hbm.at[pl.ds(pl.multiple_of(row0 + ch * rc, rc), rc), n, :]
        return pltpu.make_async_copy(ostage.at[slot], dst, osem.at[slot])

    @pl.when((j == 0) & (nzero > 0))
    def _():
        ostage[0] = jnp.zeros(ostage.shape[1:], jnp.float32)

        def start(ch, carry):
            for n in range(nn):
                out_copy(0, ch, n).start()
            return carry

        def wait(ch, carry):
            for n in range(nn):
                out_copy(0, ch, n).wait()
            return carry
        lax.fori_loop(0, nzero, start, 0)
        lax.fori_loop(0, nzero, wait, 0)

    def gather(seg, buf, wait):
        base = row0_ref[seg]

        def body(r8, carry):
            for u in range(8):
                r = r8 * 8 + u
                cp = pltpu.make_async_copy(x_hbm.at[pl.ds(tok_ref[base + r], 1)], xp.at[buf, pl.ds(r, 1)],
                                           gsem.at[buf])
                if wait:
                    cp.wait()
                else:
                    cp.start(priority=u % 2)
            return carry
        lax.fori_loop(0, nch_ref[seg] * (rc // 8), body, 0)

    @pl.when((j == 0) & (nch > 0))
    def _():
        @pl.when(s == 0)
        def _():
            gather(0, 0, False)

        gather(s, cur, True)
        nxt = jnp.minimum(s + 1, n_seg - 1)

        @pl.when((s + 1 < n_seg) & (nch_ref[nxt] > 0))
        def _():
            gather(nxt, 1 - cur, False)

    @pl.when((j < nk) & (nch > 0))
    def _():
        wgb[...] = wg_ref[...].astype(wgb.dtype)
        wub[...] = wu_ref[...].astype(wub.dtype)

    for jj in range(nk):
        @pl.when((j == jj) & (nch > 0))
        def _(jj=jj):
            def body(ch, carry):
                rows = chunk_rows(ch)
                w = xp[cur, rows, (jj // 2) * kt:(jj // 2 + 1) * kt]
                if jj % 2 == 0:
                    xf = lax.bitcast_convert_type(lax.shift_left(w, jnp.uint32(16)), jnp.float32)
                else:
                    xf = lax.bitcast_convert_type(w & jnp.uint32(0xFFFF0000), jnp.float32)
                x = xf.astype(wgb.dtype)
                a = jnp.dot(x, wgb[...], preferred_element_type=jnp.float32)
                u = jnp.dot(x, wub[...], preferred_element_type=jnp.float32)
                if jj == 0:
                    a_acc[rows, :] = a
                    u_acc[rows, :] = u
                else:
                    a_acc[rows, :] += a
                    u_acc[rows, :] += u
                return carry
            lax.fori_loop(0, nch, body, 0)

    @pl.when((j == nk - 1) & (nch > 0))
    def _():
        def body(ch, carry):
            rows = chunk_rows(ch)
            a = a_acc[rows, :]
            hbuf[rows, :] = (a * jax.nn.sigmoid(a) * u_acc[rows, :]).astype(hbuf.dtype)
            return carry
        lax.fori_loop(0, nch, body, 0)

    @pl.when((j >= nk) & (nch > 0))
    def _():
        wdb[...] = wd_ref[...].astype(wdb.dtype)

    for n in range(nn):
        @pl.when((j == nk + n) & (nch > 0))
        def _(n=n):
            def drain(m):
                def body(ch, carry):
                    out_copy(ch, ch, m).wait()
                    return carry
                lax.fori_loop(0, nch, body, 0)

            if n > 0:
                drain(n - 1)

            def body(ch, carry):
                ostage[ch] = jnp.dot(hbuf[chunk_rows(ch), :], wdb[...], preferred_element_type=jnp.float32)
                out_copy(ch, ch, n).start(priority=GATHER_DMA_PRIORITY)
                return carry
            lax.fori_loop(0, nch, body, 0)

            if n == nn - 1:
                drain(n)


def _moe_experts(xpk, kt, plan, w_gate, w_up, w_down, layer):
    tok, _, seg_e, row0, nch, nzero, R = plan
    D = 2 * xpk.shape[1]
    nk = D // kt
    F = w_gate.shape[-1]
    nt = min(MOE_NT, D)
    nn = D // nt
    rc = MOE_ROWS
    C = rc * MOE_CHUNKS
    n_seg = seg_e.shape[0]

    def k_idx(s, j, nc):
        return jnp.where(nc[s] > 0, jnp.minimum(j, nk - 1), nk - 1)

    def n_idx(s, j, nc):
        return jnp.where(nc[s] > 0, jnp.maximum(j - nk, 0), nn - 1)

    grid_spec = pltpu.PrefetchScalarGridSpec(
        num_scalar_prefetch=5, grid=(n_seg, nk + nn),
        in_specs=[
            pl.BlockSpec(memory_space=pl.ANY),
            pl.BlockSpec((None, None, kt, F), lambda s, j, se, r0, nc, nz, tk: (layer, se[s], k_idx(s, j, nc), 0)),
            pl.BlockSpec((None, None, kt, F), lambda s, j, se, r0, nc, nz, tk: (layer, se[s], k_idx(s, j, nc), 0)),
            pl.BlockSpec((None, None, F, nt), lambda s, j, se, r0, nc, nz, tk: (layer, se[s], 0, n_idx(s, j, nc))),
        ],
        out_specs=pl.BlockSpec(memory_space=pl.ANY),
        scratch_shapes=[
            pltpu.VMEM((2, C, D // 2), jnp.uint32),
            pltpu.VMEM((C, F), jnp.float32),
            pltpu.VMEM((C, F), jnp.float32),
            pltpu.VMEM((C, F), MXU_DTYPE),
            pltpu.VMEM((kt, F), MXU_DTYPE),
            pltpu.VMEM((kt, F), MXU_DTYPE),
            pltpu.VMEM((F, nt), MXU_DTYPE),
            pltpu.VMEM((MOE_CHUNKS, rc, nt), jnp.float32),
            pltpu.SemaphoreType.DMA((2,)),
            pltpu.SemaphoreType.DMA((MOE_CHUNKS,)),
        ])
    return pl.pallas_call(
        functools.partial(_moe_kernel, rc=rc, nk=nk, nn=nn, n_seg=n_seg), grid_spec=grid_spec,
        out_shape=jax.ShapeDtypeStruct((R, nn, nt), jnp.float32),
        compiler_params=_cparams("arbitrary", "arbitrary"), name="moe_experts",
    )(seg_e, row0, nch, nzero, tok, xpk, w_gate, w_up, w_down)


def _combine_kernel(dest_ref, x_ref, gate_ref, gam_ref, bet_ref, y_hbm, o32_ref, o16_ref, ybuf, sem,
                    *, tc, alpha, n_tiles):
    i = pl.program_id(0)
    cur = i % 2

    def rows(tile, buf, wait):
        def body(r, carry):
            for k in range(EXPERT_TOPK):
                slot = dest_ref[(tile * tc + r) * EXPERT_TOPK + k]
                cp = pltpu.make_async_copy(y_hbm.at[slot], ybuf.at[buf, k, r], sem.at[buf])
                if wait:
                    cp.wait()
                else:
                    cp.start(priority=k % 2)
            return carry
        lax.fori_loop(0, tc, body, 0)

    @pl.when(i == 0)
    def _():
        rows(0, 0, False)

    rows(i, cur, True)

    @pl.when(i + 1 < n_tiles)
    def _():
        rows(i + 1, 1 - cur, False)

    gate = gate_ref[...]
    nn, nt = ybuf.shape[3], ybuf.shape[4]
    parts = []
    for n in range(nn):
        ffn = gate[:, 0:1] * ybuf[cur, 0, :, n, :]
        for k in range(1, EXPERT_TOPK):
            ffn = ffn + gate[:, k:k + 1] * ybuf[cur, k, :, n, :]
        parts.append(alpha * x_ref[:, n * nt:(n + 1) * nt] + ffn)
    v = jnp.concatenate(parts, axis=1)
    mu = jnp.mean(v, axis=-1, keepdims=True)
    d = v - mu
    var = jnp.mean(d * d, axis=-1, keepdims=True)
    y = d * lax.rsqrt(var + LN_EPS) * gam_ref[...] + bet_ref[...]
    o32_ref[...] = y
    o16_ref[...] = y.astype(o16_ref.dtype)


def _moe_combine(dest, x32, gates, y_sorted, gam, bet, alpha):
    Np, D = x32.shape
    tc = _pick_tile(Np, 192, 16)
    row = lambda i, d: (i, 0)
    grid_spec = pltpu.PrefetchScalarGridSpec(
        num_scalar_prefetch=1, grid=(Np // tc,),
        in_specs=[
            pl.BlockSpec((tc, D), row),
            pl.BlockSpec((tc, LANE), row),
            pl.BlockSpec((1, D), lambda i, d: (0, 0)),
            pl.BlockSpec((1, D), lambda i, d: (0, 0)),
            pl.BlockSpec(memory_space=pl.ANY),
        ],
        out_specs=[pl.BlockSpec((tc, D), row), pl.BlockSpec((tc, D), row)],
        scratch_shapes=[pltpu.VMEM((2, EXPERT_TOPK, tc) + y_sorted.shape[1:], jnp.float32),
                        pltpu.SemaphoreType.DMA((2,))])
    return pl.pallas_call(
        functools.partial(_combine_kernel, tc=tc, alpha=alpha, n_tiles=Np // tc), grid_spec=grid_spec,
        out_shape=[jax.ShapeDtypeStruct((Np, D), jnp.float32), jax.ShapeDtypeStruct((Np, D), MXU_DTYPE)],
        compiler_params=_cparams("arbitrary"), name="moe_combine",
    )(dest, x32, gates, gam.reshape(1, D), bet.reshape(1, D), y_sorted)


def _rope_tables(pos, scale_wi):
    def cs(dim):
        half = dim // 2
        inv = ROPE_THETA ** (-jnp.arange(half, dtype=jnp.float32) * 2.0 / dim)
        ang = pos.astype(jnp.float32)[:, None] * inv[None, :]
        return jnp.cos(ang), jnp.sin(ang)
    c, s = cs(HEAD_DIM)
    t128 = (jnp.concatenate([c, c], -1), jnp.concatenate([-s, s], -1))
    c, s = cs(IDX_DIM)
    c64 = jnp.concatenate([c, c], -1)
    s64 = jnp.concatenate([-s, s], -1)
    t64 = (jnp.tile(c64, (1, LANE // IDX_DIM)), jnp.tile(s64, (1, LANE // IDX_DIM)))
    n = pos.shape[0]
    pad = LANE - IDX_DIM - IDX_HEADS
    tkw = (jnp.concatenate([c64, jnp.full((n, IDX_HEADS), scale_wi, jnp.float32), jnp.zeros((n, pad), jnp.float32)], -1),
           jnp.concatenate([s64, jnp.zeros((n, LANE - IDX_DIM), jnp.float32)], -1))
    return t128, t64, tkw


def kernel(x_prompt, x_sample, cache_a_k, cache_a_v, cache_a_kidx, cache_b_k, cache_b_v, page_table, p_prompt, p_sample, ln_emb_g, ln_emb_b, w_in, w_out, ln1_g, ln1_b, w_route_group, w_route_expert, w_exp_gate, w_exp_up, w_exp_down, ln2_g, ln2_b, w_ple, w_ple_gate, ln3_g, ln3_b):
    B, T, D = x_prompt.shape
    Bd, Td, _ = x_sample.shape
    depth = w_in.shape[0]
    n_phys, page = cache_a_k.shape[1], cache_a_k.shape[2]
    past = page_table.shape[1] * page
    n_experts = w_exp_gate.shape[1]
    h_a = D // 2 // HEAD_DIM
    h_b = h_a
    rep_a, rep_b = h_a // KV_A, h_b // KV_B
    assert IDX_DIM * 2 == LANE and IDX_DIM + IDX_HEADS <= LANE and page == LANE
    assert N_GROUPS * (1 + EXPERTS_PER_GROUP) <= LANE and n_experts == N_GROUPS * EXPERTS_PER_GROUP
    alpha = (2.0 * depth) ** 0.25
    n_p, n_s = B * T, Bd * Td
    N = n_p + n_s
    Np = _round_up(N, ROW_ALIGN)
    f32 = jnp.float32

    def stream(a_p, a_s):
        w = a_p.shape[-1]
        return jnp.concatenate([a_p.reshape(n_p, w), a_s.reshape(n_s, w), jnp.zeros((Np - N, w), a_p.dtype)], 0)

    pos = jnp.concatenate([jnp.tile(jnp.arange(T, dtype=jnp.int32), B),
                           jnp.tile(past + jnp.arange(Td, dtype=jnp.int32), Bd),
                           jnp.zeros((Np - N,), jnp.int32)])
    t128, t64, tkw = _rope_tables(pos, IDX_HEADS ** -0.5 * IDX_DIM ** -0.5)

    widths = (h_a * HEAD_DIM, KV_A * HEAD_DIM, KV_A * HEAD_DIM, IDX_HEADS * IDX_DIM, IDX_DIM, IDX_HEADS,
              h_b * HEAD_DIM, KV_B * HEAD_DIM, KV_B * HEAD_DIM)
    offs = [0]
    for w in widths:
        offs.append(offs[-1] + w)
    assert offs[-1] == w_in.shape[2]

    def cols(w, *ids):
        return jnp.concatenate([w[:, offs[i]:offs[i + 1]] for i in ids], axis=1).astype(MXU_DTYPE)

    kv_w = KV_A * HEAD_DIM
    trow = jnp.arange(Td, dtype=jnp.int32)[:, None]
    own_bias = jnp.where(jnp.arange(LANE, dtype=jnp.int32)[None, :] <= trow, 0.0, NEG_BIAS).astype(f32)[None]

    def fresh(rows):
        w = rows.shape[-1]
        return jnp.pad(rows.reshape(Bd, Td, w), ((0, 0), (0, LANE - Td), (0, 0)))

    def decode_q(q_sm, kv, rep):
        q = q_sm.reshape(kv, rep, Bd, Td, HEAD_DIM)
        return jnp.transpose(q, (2, 0, 1, 3, 4)).reshape(Bd, kv, rep * Td, HEAD_DIM)

    def decode_o(o, kv, rep):
        o = o.reshape(Bd, kv, rep, Td, HEAD_DIM)
        return jnp.transpose(o, (0, 3, 1, 2, 4)).reshape(n_s, kv * rep * HEAD_DIM)

    x_tail = jnp.concatenate([x_sample.reshape(n_s, D), jnp.zeros((Np - N, D), x_sample.dtype)], axis=0)
    h32, h16 = _layer_norm_embed(x_prompt.reshape(n_p, D), x_tail, ln_emb_g, ln_emb_b)
    rows_out = []
    for l in range(depth):
        wl = w_in[l]
        q128 = _matmul([h16], [cols(wl, 0, 6)], MXU_DTYPE, "rope128", t128, True, name="proj_q")
        k128 = _matmul([h16], [cols(wl, 1, 7)], f32, "rope128", t128, name="proj_k")
        v128 = _matmul([h16], [cols(wl, 2, 8)], f32, name="proj_v")
        qi = _matmul([h16], [cols(wl, 3)], MXU_DTYPE, "rope64", t64, True, name="proj_qi")
        w_kw = jnp.pad(cols(wl, 4, 5), ((0, 0), (0, LANE - IDX_DIM - IDX_HEADS)))
        kiwi = _matmul([h16], [w_kw], f32, "rope64", tkw, name="proj_kiwi")

        attn_a = _dsa_prompt(q128, k128, v128, qi, kiwi, B, T, rep_a)
        attn_b = _moba_prompt(q128, k128, v128, B, T, rep_b, h_a)

        ks, vs = k128[n_p:N], v128[n_p:N]
        kiwi_s = kiwi[n_p:N]
        qi_s = qi[:, n_p:N].reshape(IDX_HEADS // 2, Bd, Td, 2, IDX_DIM)
        qi_s = jnp.transpose(qi_s, (1, 0, 3, 2, 4)).reshape(Bd, IDX_HEADS * Td, IDX_DIM)
        wcol = jnp.transpose(kiwi_s[:, IDX_DIM:IDX_DIM + IDX_HEADS].reshape(Bd, Td, IDX_HEADS), (0, 2, 1))
        wcol = wcol.reshape(Bd, IDX_HEADS * Td, 1)
        bias_p, bias_n = _dsa_decode_mask(page_table, qi_s, wcol, cache_a_kidx, fresh(kiwi_s[:, :IDX_DIM]), l, Td)
        o_a = _paged_attention(page_table, decode_q(q128[:h_a, n_p:N], KV_A, rep_a), cache_a_k, cache_a_v,
                               fresh(ks[:, :kv_w]), fresh(vs[:, :kv_w]), bias_p[:, :, None], bias_n, l, Td, rep_a)
        qb_s = decode_q(q128[h_a:, n_p:N], KV_B, rep_b)
        bias_b = _moba_decode_mask(page_table, qb_s, cache_b_k, l, Td, rep_b)
        o_b = _paged_attention(page_table, qb_s, cache_b_k, cache_b_v, fresh(ks[:, kv_w:]), fresh(vs[:, kv_w:]),
                               bias_b, own_bias, l, Td, rep_b)
        tail = jnp.zeros((Np - N, h_a * HEAD_DIM), MXU_DTYPE)
        attn_a = jnp.concatenate([attn_a, decode_o(o_a, KV_A, rep_a).astype(MXU_DTYPE), tail], axis=0)
        attn_b = jnp.concatenate([attn_b, decode_o(o_b, KV_B, rep_b).astype(MXU_DTYPE), tail], axis=0)

        w_o = w_out[l].astype(MXU_DTYPE)
        mix = _matmul([attn_a, attn_b], [w_o[:h_a * HEAD_DIM], w_o[h_a * HEAD_DIM:]], f32, name="proj_out")
        w_r = jnp.pad(jnp.concatenate([w_route_group[l], w_route_expert[l]], axis=1),
                      ((0, 0), (0, LANE - N_GROUPS - n_experts)))
        w_rh = w_r.astype(MXU_DTYPE)
        w_rl = (w_r - w_rh.astype(f32)).astype(MXU_DTYPE)
        moe_kt = min(MOE_KT, D // 2)
        x32, xpk, logits = _layer_norm([h32, mix], ln1_g[l], ln1_b[l], "add", alpha, (w_rh, w_rl), pack_kt=moe_kt)

        eid, gates = _route(logits)
        plan = _moe_plan(eid[:N, :EXPERT_TOPK].reshape(-1), n_experts, MOE_ROWS, MOE_CHUNKS)
        y_sorted = _moe_experts(xpk, moe_kt, plan, w_exp_gate, w_exp_up, w_exp_down, l)
        dest = jnp.pad(plan[1], (0, (Np - N) * EXPERT_TOPK))
        x32, x16 = _moe_combine(dest, x32, gates, y_sorted, ln2_g[l], ln2_b[l], alpha)

        gate_pre = _matmul([x16], [w_ple_gate[l].astype(MXU_DTYPE)], f32, name="ple_gate")
        p16 = stream(p_prompt[l], p_sample[l]).astype(MXU_DTYPE)
        ple = _matmul([p16], [w_ple[l].astype(MXU_DTYPE)], f32, name="ple_embed")
        h32, h16 = _layer_norm([x32, gate_pre, ple], ln3_g[l], ln3_b[l], "ple", alpha)
        rows_out.append((k128, v128, kiwi))

    def gather_rows(sel, lo, hi, lead):
        return jnp.stack([sel(r)[lo:hi].reshape(lead) for r in rows_out])

    outs = [h32[:n_p].reshape(B, T, D), h32[n_p:N].reshape(Bd, Td, D)]
    for lo, hi, lead in ((0, n_p, (B, T)), (n_p, N, (Bd, Td))):
        outs += [
            gather_rows(lambda r: r[0][:, :kv_w], lo, hi, lead + (KV_A, HEAD_DIM)),
            gather_rows(lambda r: r[1][:, :kv_w], lo, hi, lead + (KV_A, HEAD_DIM)),
            gather_rows(lambda r: r[2][:, :IDX_DIM], lo, hi, lead + (IDX_DIM,)),
            gather_rows(lambda r: r[0][:, kv_w:], lo, hi, lead + (KV_B, HEAD_DIM)),
            gather_rows(lambda r: r[1][:, kv_w:], lo, hi, lead + (KV_B, HEAD_DIM)),
        ]
    return tuple(outs)
```
